```python
import math
import jax, jax.numpy as jnp
from jax import lax
import numpy as np

D_MODEL = 1024
BATCH = 8
SEQ = 2048
DEPTH = 4
DEC_BATCH = 32
DEC_SEQ = 4
PAST_LEN = 8192
PAGE_SIZE = 128

N_MIXERS = 2
N_ATTN_LAYERS = (DEPTH + N_MIXERS - 1) // N_MIXERS
N_SSM_LAYERS = DEPTH // N_MIXERS
N_HEADS = 8
HEAD_DIM = 64
V_DIM = 2 * HEAD_DIM
QKV_COLS = 3 * N_HEADS * 2 * HEAD_DIM
GROUP = 16
N_GROUPS = D_MODEL // GROUP
STATE = 64
FFN_HIDDEN = -(-8 * D_MODEL // (3 * 256)) * 256
PLE_DIM = 256
Q_BLOCK = 128
EPS = 1e-6
LOG_DT_MIN = math.log(0.001)
LOG_DT_MAX = math.log(0.1)

kernel_name = "hybrid_diffattn_s5_step"


def rms_norm(x, g):
    xf = x.astype(jnp.float32)
    y = xf * lax.rsqrt(jnp.mean(xf * xf, axis=-1, keepdims=True) + EPS)
    return (y * g.astype(jnp.float32)).astype(x.dtype)


def alibi_slopes():
    return 2.0 ** (-8.0 * jnp.arange(1, N_HEADS + 1, dtype=jnp.float32) / N_HEADS)


def lambda_init(layer):
    return 0.8 - 0.6 * math.exp(-0.3 * layer)


def diff_lambda(lq1, lk1, lq2, lk2, lam_init):
    f32 = jnp.float32
    return (jnp.exp(jnp.sum(lq1.astype(f32) * lk1.astype(f32)))
            - jnp.exp(jnp.sum(lq2.astype(f32) * lk2.astype(f32))) + lam_init)


def diff_qkv(h, w_qkv, q_norm, k_norm):
    B, T, _ = h.shape
    q, k, v = jnp.split(h @ w_qkv, 3, axis=-1)
    q = rms_norm(q.reshape(B, T, N_HEADS, 2, HEAD_DIM), q_norm)
    k = rms_norm(k.reshape(B, T, N_HEADS, 2, HEAD_DIM), k_norm)
    return q, k, v.reshape(B, T, N_HEADS, V_DIM)


def diff_attend(q, q_pos, segs, lam):
    slopes = alibi_slopes()
    scores = []
    for k, _, k_pos in segs:
        s = jnp.einsum("bqhcd,bkhcd->bhcqk", q, k).astype(jnp.float32) * (HEAD_DIM ** -0.5)
        dist = (q_pos[:, None] - k_pos[None, :]).astype(jnp.float32)
        s = s - slopes[None, :, None, None, None] * dist
        scores.append(jnp.where(dist >= 0, s, -jnp.inf))
    prob = jax.nn.softmax(jnp.concatenate(scores, axis=-1), axis=-1)
    w = prob[:, :, 0] - lam * prob[:, :, 1]
    out = None
    off = 0
    for _, v, k_pos in segs:
        n = k_pos.shape[0]
        o = jnp.einsum("bhqk,bkhe->bqhe", w[..., off:off + n].astype(v.dtype), v)
        out = o if out is None else out + o
        off += n
    return out


def diff_attn_prompt(q, k, v, lam):
    B, T = q.shape[:2]
    nb = T // Q_BLOCK
    qb = q.reshape(B, nb, Q_BLOCK, N_HEADS, 2, HEAD_DIM).transpose(1, 0, 2, 3, 4, 5)
    k_pos = jnp.arange(T, dtype=jnp.int32)

    def block(args):
        qi, bi = args
        q_pos = bi * Q_BLOCK + jnp.arange(Q_BLOCK, dtype=jnp.int32)
        return diff_attend(qi, q_pos, [(k, v, k_pos)], lam)

    o = lax.map(block, (qb, jnp.arange(nb, dtype=jnp.int32)))
    return o.transpose(1, 0, 2, 3, 4).reshape(B, T, N_HEADS, V_DIM)


def diff_attn_sample(q, k_new, v_new, k_pool, v_pool, page_table, lam):
    DB, Tn = q.shape[:2]
    past = page_table.shape[1] * PAGE_SIZE
    k_past = k_pool[page_table].reshape(DB, past, N_HEADS, 2, HEAD_DIM)
    v_past = v_pool[page_table].reshape(DB, past, N_HEADS, V_DIM)
    q_pos = past + jnp.arange(Tn, dtype=jnp.int32)
    segs = [(k_past, v_past, jnp.arange(past, dtype=jnp.int32)), (k_new, v_new, q_pos)]
    return diff_attend(q, q_pos, segs, lam)


def diff_attn_out(o, subln, w_o, lam_init):
    B, T = o.shape[:2]
    o = rms_norm(o, subln) * (1.0 - lam_init)
    return o.reshape(B, T, N_HEADS * V_DIM) @ w_o


def s5_ssm(u, h0_re, h0_im, a_re, a_im, log_step, b_re, b_im, c_re, c_im, d_skip):
    f32 = jnp.float32
    B, T, _ = u.shape
    uf = u.astype(f32)
    ar = a_re.astype(f32)
    ai = a_im.astype(f32)
    dt = jnp.exp(log_step.astype(f32))[:, None]
    mag = jnp.exp(ar * dt)
    ang = ai * dt
    abr = mag * jnp.cos(ang)
    abi = mag * jnp.sin(ang)
    den = ar * ar + ai * ai
    nr = abr - 1.0
    fr = (nr * ar + abi * ai) / den
    fi = (abi * ar - nr * ai) / den
    br = b_re.astype(f32)
    bi = b_im.astype(f32)
    bbr = fr[..., None] * br - fi[..., None] * bi
    bbi = fr[..., None] * bi + fi[..., None] * br
    ug = uf.reshape(B, T, N_GROUPS, GROUP)
    xr = jnp.einsum("btgc,gpc->tbgp", ug, bbr)
    xi = jnp.einsum("btgc,gpc->tbgp", ug, bbi)
    if h0_re is not None:
        h0r = h0_re.astype(f32)
        h0i = h0_im.astype(f32)
        xr = xr.at[0].add(abr * h0r - abi * h0i)
        xi = xi.at[0].add(abr * h0i + abi * h0r)
    a_r = jnp.broadcast_to(abr, (T, 1, N_GROUPS, STATE))
    a_i = jnp.broadcast_to(abi, (T, 1, N_GROUPS, STATE))

    def combine(e1, e2):
        a1r, a1i, b1r, b1i = e1
        a2r, a2i, b2r, b2i = e2
        return (a2r * a1r - a2i * a1i,
                a2r * a1i + a2i * a1r,
                a2r * b1r - a2i * b1i + b2r,
                a2r * b1i + a2i * b1r + b2i)

    _, _, hr, hi = lax.associative_scan(combine, (a_r, a_i, xr, xi), axis=0)
    y = (jnp.einsum("tbgp,gcp->btgc", hr, c_re.astype(f32))
         - jnp.einsum("tbgp,gcp->btgc", hi, c_im.astype(f32)))
    y = y.reshape(B, T, D_MODEL) + d_skip.astype(f32) * uf
    return y.astype(u.dtype), hr[-1], hi[-1]


def swiglu(x, w_gate, w_up, w_down):
    return (jax.nn.silu(x @ w_gate) * (x @ w_up)) @ w_down


def setup_inputs(seed: int = 0) -> dict:
    key = jax.random.key(seed)
    ks = iter(jax.random.split(key, 48))
    nrm = lambda shape, scale: jax.random.normal(next(ks), shape, jnp.float32) * scale
    gain = lambda shape: 1.0 + nrm(shape, 0.02)
    n_pages = PAST_LEN // PAGE_SIZE
    n_used = DEC_BATCH * n_pages
    n_pool = n_used + (n_used + 3) // 4
    x_prompt = nrm((BATCH, SEQ, D_MODEL), 1.0)
    x_sample = nrm((DEC_BATCH, DEC_SEQ, D_MODEL), 1.0)
    cache_k = nrm((N_ATTN_LAYERS, n_pool, PAGE_SIZE, N_HEADS, 2 * HEAD_DIM), 1.0)
    cache_v = nrm((N_ATTN_LAYERS, n_pool, PAGE_SIZE, N_HEADS, V_DIM), 1.0)
    state_ssm_re = nrm((N_SSM_LAYERS, DEC_BATCH, N_GROUPS, STATE), 0.5)
    state_ssm_im = nrm((N_SSM_LAYERS, DEC_BATCH, N_GROUPS, STATE), 0.5)
    page_table = jax.random.permutation(next(ks), n_pool)[:n_used].reshape(DEC_BATCH, n_pages).astype(jnp.int32)
    p_prompt = nrm((DEPTH, BATCH, SEQ, PLE_DIM), 1.0)
    p_sample = nrm((DEPTH, DEC_BATCH, DEC_SEQ, PLE_DIM), 1.0)
    ssm_a_re = -0.5 * jnp.exp(nrm((N_SSM_LAYERS, N_GROUPS, STATE), 0.02))
    ssm_a_im = (math.pi * jnp.broadcast_to(jnp.arange(STATE, dtype=jnp.float32), (N_SSM_LAYERS, N_GROUPS, STATE))
                + nrm((N_SSM_LAYERS, N_GROUPS, STATE), 0.01))
    ssm_log_step = jax.random.uniform(next(ks), (N_SSM_LAYERS, N_GROUPS), jnp.float32, LOG_DT_MIN, LOG_DT_MAX)
    return {
        "x_prompt": x_prompt,
        "x_sample": x_sample,
        "cache_k": cache_k,
        "cache_v": cache_v,
        "state_ssm_re": state_ssm_re,
        "state_ssm_im": state_ssm_im,
        "page_table": page_table,
        "p_prompt": p_prompt,
        "p_sample": p_sample,
        "norm_mix": gain((DEPTH, D_MODEL)),
        "norm_ffn": gain((DEPTH, D_MODEL)),
        "norm_ple": gain((DEPTH, D_MODEL)),
        "w_qkv": nrm((N_ATTN_LAYERS, D_MODEL, QKV_COLS), D_MODEL ** -0.5),
        "q_norm": gain((N_ATTN_LAYERS, HEAD_DIM)),
        "k_norm": gain((N_ATTN_LAYERS, HEAD_DIM)),
        "lam_q1": nrm((N_ATTN_LAYERS, HEAD_DIM), 0.1),
        "lam_k1": nrm((N_ATTN_LAYERS, HEAD_DIM), 0.1),
        "lam_q2": nrm((N_ATTN_LAYERS, HEAD_DIM), 0.1),
        "lam_k2": nrm((N_ATTN_LAYERS, HEAD_DIM), 0.1),
        "subln": gain((N_ATTN_LAYERS, V_DIM)),
        "w_o": nrm((N_ATTN_LAYERS, N_HEADS * V_DIM, D_MODEL), (N_HEADS * V_DIM) ** -0.5),
        "ssm_a_re": ssm_a_re,
        "ssm_a_im": ssm_a_im,
        "ssm_log_step": ssm_log_step,
        "ssm_b_re": nrm((N_SSM_LAYERS, N_GROUPS, STATE, GROUP), (2 * GROUP) ** -0.5),
        "ssm_b_im": nrm((N_SSM_LAYERS, N_GROUPS, STATE, GROUP), (2 * GROUP) ** -0.5),
        "ssm_c_re": nrm((N_SSM_LAYERS, N_GROUPS, GROUP, STATE), (2 * STATE) ** -0.5),
        "ssm_c_im": nrm((N_SSM_LAYERS, N_GROUPS, GROUP, STATE), (2 * STATE) ** -0.5),
        "ssm_d": nrm((N_SSM_LAYERS, D_MODEL), 1.0),
        "w_glu_a": nrm((N_SSM_LAYERS, D_MODEL, D_MODEL), D_MODEL ** -0.5),
        "w_glu_b": nrm((N_SSM_LAYERS, D_MODEL, D_MODEL), D_MODEL ** -0.5),
        "w_ffn_gate": nrm((DEPTH, D_MODEL, FFN_HIDDEN), D_MODEL ** -0.5),
        "w_ffn_up": nrm((DEPTH, D_MODEL, FFN_HIDDEN), D_MODEL ** -0.5),
        "w_ffn_down": nrm((DEPTH, FFN_HIDDEN, D_MODEL), FFN_HIDDEN ** -0.5),
        "w_ple_proj": nrm((DEPTH, PLE_DIM, D_MODEL), PLE_DIM ** -0.5),
        "w_ple_gate": nrm((DEPTH, D_MODEL, D_MODEL), D_MODEL ** -0.5),
    }


def reference(x_prompt, x_sample, cache_k, cache_v, state_ssm_re, state_ssm_im, page_table,
              p_prompt, p_sample, norm_mix, norm_ffn, norm_ple, w_qkv, q_norm, k_norm,
              lam_q1, lam_k1, lam_q2, lam_k2, subln, w_o, ssm_a_re, ssm_a_im, ssm_log_step,
              ssm_b_re, ssm_b_im, ssm_c_re, ssm_c_im, ssm_d, w_glu_a, w_glu_b,
              w_ffn_gate, w_ffn_up, w_ffn_down, w_ple_proj, w_ple_gate):

    def trunk(x, p, is_sample):
        B, T, _ = x.shape
        k_rows, v_rows, s_re, s_im = [], [], [], []
        h = x
        for i in range(DEPTH):
            hn = rms_norm(h, norm_mix[i])
            j = i // N_MIXERS
            if i % N_MIXERS == 0:
                q, k, v = diff_qkv(hn, w_qkv[j], q_norm[j], k_norm[j])
                lam0 = lambda_init(i)
                lam = diff_lambda(lam_q1[j], lam_k1[j], lam_q2[j], lam_k2[j], lam0)
                if is_sample:
                    o = diff_attn_sample(q, k, v, cache_k[j], cache_v[j], page_table, lam)
                else:
                    o = diff_attn_prompt(q, k, v, lam)
                mix = diff_attn_out(o, subln[j], w_o[j], lam0)
                k_rows.append(k.reshape(B, T, N_HEADS, 2 * HEAD_DIM))
                v_rows.append(v)
            else:
                h0r = state_ssm_re[j] if is_sample else None
                h0i = state_ssm_im[j] if is_sample else None
                y, hr, hi = s5_ssm(hn, h0r, h0i, ssm_a_re[j], ssm_a_im[j], ssm_log_step[j],
                                   ssm_b_re[j], ssm_b_im[j], ssm_c_re[j], ssm_c_im[j], ssm_d[j])
                g = jax.nn.gelu(y)
                mix = (g @ w_glu_a[j]) * jax.nn.sigmoid(g @ w_glu_b[j])
                s_re.append(hr)
                s_im.append(hi)
            h = h + mix
            h = h + swiglu(rms_norm(h, norm_ffn[i]), w_ffn_gate[i], w_ffn_up[i], w_ffn_down[i])
            gate = jax.nn.sigmoid(rms_norm(h, norm_ple[i]) @ w_ple_gate[i])
            h = h + gate * (p[i] @ w_ple_proj[i])
        return h, jnp.stack(k_rows), jnp.stack(v_rows), jnp.stack(s_re), jnp.stack(s_im)

    y_prompt, k_prompt, v_prompt, ssm_re_prompt, ssm_im_prompt = trunk(x_prompt, p_prompt, False)
    y_sample, k_sample, v_sample, ssm_re_sample, ssm_im_sample = trunk(x_sample, p_sample, True)
    return (y_prompt, y_sample, k_prompt, v_prompt, k_sample, v_sample,
            ssm_re_prompt, ssm_im_prompt, ssm_re_sample, ssm_im_sample)
```

```python
import functools
import math

import jax
import jax.numpy as jnp
from jax import lax
from jax.experimental import pallas as pl
from jax.experimental.pallas import tpu as pltpu

F32 = jnp.float32
BF16 = jnp.bfloat16

D_MODEL = 1024
N_HEADS = 8
HEAD_DIM = 64
V_DIM = 2 * HEAD_DIM
GROUP = 16
STATE = 64
N_GROUPS = D_MODEL // GROUP
N_STATE = N_GROUPS * STATE
PAGE_SIZE = 128
DEPTH = 4
EPS = 1e-6
NEG = -1e30

LANES = 128
SUBLANES = 8
MXU_DIM = 256
VMEM_LIMIT = 56 * 1024 * 1024

CH_CHUNK = MXU_DIM
ST_CHUNK = CH_CHUNK // GROUP * STATE
NEW_KEYS_PAD = LANES // N_HEADS


def _lambda_init(layer):
    return 0.8 - 0.6 * math.exp(-0.3 * layer)


def _alibi_slopes():
    return 2.0 ** (-8.0 * jnp.arange(1, N_HEADS + 1, dtype=F32) / N_HEADS)


def _rms(x, gain):
    return x * lax.rsqrt(jnp.mean(x * x, axis=-1, keepdims=True) + EPS) * gain


def _resident(shape):
    nd = len(shape)
    return pl.BlockSpec(shape, lambda *_: (0,) * nd, pipeline_mode=pl.Buffered(1))


def _params(*sem):
    return pltpu.CompilerParams(dimension_semantics=sem, vmem_limit_bytes=VMEM_LIMIT)


def _row_view(x, layout, batch):
    rows, cols = x.shape
    return x if layout == "bm" else x.reshape(rows // batch, batch * cols)


def _row_spec(layout, tm, cols, n_t):
    if layout == "bm":
        return pl.BlockSpec((tm, cols), lambda i: (i, 0))
    return pl.BlockSpec((tm, cols), lambda i: (i % n_t, i // n_t))


def _qkv_body(h_ref, g_ref, w_ref, qn_ref, kn_ref, e_ref,
              q_ref, k_ref, v_ref, kb_ref, vb_ref):
    hn = _rms(h_ref[...], g_ref[...]).astype(BF16)
    qkv = jnp.dot(hn, w_ref[...], preferred_element_type=F32)
    cols = q_ref.shape[1]

    def head_norm(z, gain):
        sq = (z * z).astype(BF16)
        parts = [jnp.dot(sq[:, j * MXU_DIM:(j + 1) * MXU_DIM], e_ref[...], preferred_element_type=F32)
                 for j in range(cols // MXU_DIM)]
        ss = jnp.concatenate(parts, axis=1)
        return z * lax.rsqrt(ss * (1.0 / HEAD_DIM) + EPS) * gain

    q = head_norm(qkv[:, :cols], qn_ref[...]) * (HEAD_DIM ** -0.5)
    k = head_norm(qkv[:, cols:2 * cols], kn_ref[...])
    v = qkv[:, 2 * cols:]
    q_ref[...] = q.astype(BF16)
    k_ref[...] = k
    v_ref[...] = v
    kb_ref[...] = k.astype(BF16)
    vb_ref[...] = v.astype(BF16)


def _qkv(h, gain, w, qn, kn, tm):
    m = h.shape[0]
    cols = N_HEADS * 2 * HEAD_DIM
    ones_bd = (jnp.arange(MXU_DIM)[:, None] // HEAD_DIM == jnp.arange(MXU_DIM)[None, :] // HEAD_DIM).astype(BF16)
    row = lambda c: pl.BlockSpec((tm, c), lambda i: (i, 0))
    return pl.pallas_call(
        _qkv_body,
        grid=(m // tm,),
        in_specs=[row(D_MODEL), _resident((1, D_MODEL)), _resident(w.shape),
                  _resident((1, cols)), _resident((1, cols)), _resident((MXU_DIM, MXU_DIM))],
        out_specs=[row(cols)] * 5,
        out_shape=[jax.ShapeDtypeStruct((m, cols), BF16),
                   jax.ShapeDtypeStruct((m, cols), F32),
                   jax.ShapeDtypeStruct((m, cols), F32),
                   jax.ShapeDtypeStruct((m, cols), BF16),
                   jax.ShapeDtypeStruct((m, cols), BF16)],
        compiler_params=_params("parallel"),
        name="qkv",
    )(h, gain.reshape(1, -1), w, jnp.tile(qn, cols // HEAD_DIM).reshape(1, -1),
      jnp.tile(kn, cols // HEAD_DIM).reshape(1, -1), ones_bd)


def _diff_lambda(lam_ref, lam_init):
    l = lam_ref[...]
    a = jnp.sum(l[0:1] * l[1:2], axis=-1, keepdims=True)
    b = jnp.sum(l[2:3] * l[3:4], axis=-1, keepdims=True)
    return jnp.exp(a) - jnp.exp(b) + lam_init


def _online_softmax_step(s, v_bf, m_ref, l_ref, acc_ref, offset=None):
    m_prev = m_ref[...]
    m_cur = jnp.max(s, axis=-1, keepdims=True)
    if offset is not None:
        m_cur = m_cur + offset
    m_new = jnp.maximum(m_prev, m_cur)
    shift = m_new if offset is None else m_new - offset
    p = jnp.exp(s - shift)
    alpha = jnp.exp(m_prev - m_new)
    l_ref[...] = alpha * l_ref[...] + jnp.sum(p, axis=-1, keepdims=True)
    acc_ref[...] = alpha * acc_ref[...] + jnp.dot(p.astype(BF16), v_bf, preferred_element_type=F32)
    m_ref[...] = m_new


def _diff_finish(acc_ref, l_ref, half, lam, sub_ref, lam_init):
    o1 = acc_ref[0:half, :] / l_ref[0:half, :]
    o2 = acc_ref[half:2 * half, :] / l_ref[half:2 * half, :]
    o = o1 - lam * o2
    return _rms(o, sub_ref[...]) * (1.0 - lam_init)


def _attn_prompt_body(q_ref, k_ref, v_ref, slope_ref, lam_ref, sub_ref, o_ref,
                      m_ref, l_ref, acc_ref, *, tq, tk, lam_init):
    qi = pl.program_id(2)
    q = q_ref[...]
    lane = lax.broadcasted_iota(jnp.int32, q.shape, 1)
    zero = jnp.zeros_like(q)
    q2 = jnp.concatenate([jnp.where(lane < HEAD_DIM, q, zero),
                          jnp.where(lane >= HEAD_DIM, q, zero)], axis=0)
    slope = slope_ref[0]
    m_ref[...] = jnp.full(m_ref.shape, NEG, F32)
    l_ref[...] = jnp.zeros(l_ref.shape, F32)
    acc_ref[...] = jnp.zeros(acc_ref.shape, F32)
    q_pos = qi * tq + lax.broadcasted_iota(jnp.int32, (tq, tk), 0)
    k_off = lax.broadcasted_iota(jnp.int32, (tq, tk), 1)

    def block(j, carry):
        start = pl.multiple_of(j * tk, tk)
        kb = k_ref[pl.ds(start, tk), :]
        vb = v_ref[pl.ds(start, tk), :]
        s = lax.dot_general(q2, kb, (((1,), (1,)), ((), ())), preferred_element_type=F32)
        dist = q_pos - (k_off + j * tk)
        bias = jnp.where(dist >= 0, -slope[:, 0:1] * dist.astype(F32), NEG)
        s = s + jnp.concatenate([bias, bias], axis=0)
        _online_softmax_step(s, vb, m_ref, l_ref, acc_ref)
        return carry

    lax.fori_loop(0, ((qi + 1) * tq + tk - 1) // tk, block, 0)
    lam = _diff_lambda(lam_ref, lam_init)
    o_ref[...] = _diff_finish(acc_ref, l_ref, tq, lam, sub_ref, lam_init).astype(BF16)


def _attn_prompt(q, kb, vb, lam_params, subln, lam_init, batch, seq, tq, tk):
    nq = seq // tq
    slopes = jnp.broadcast_to(_alibi_slopes()[:, None, None], (N_HEADS, 1, LANES))
    body = functools.partial(_attn_prompt_body, tq=tq, tk=tk, lam_init=lam_init)
    return pl.pallas_call(
        body,
        grid=(batch, N_HEADS, nq),
        in_specs=[pl.BlockSpec((tq, V_DIM), lambda b, h, i: (b * nq + i, h)),
                  pl.BlockSpec((seq, V_DIM), lambda b, h, i: (b, h)),
                  pl.BlockSpec((seq, V_DIM), lambda b, h, i: (b, h)),
                  pl.BlockSpec((1, 1, LANES), lambda b, h, i: (h, 0, 0)),
                  pl.BlockSpec((4, HEAD_DIM), lambda b, h, i: (0, 0)),
                  pl.BlockSpec((1, V_DIM), lambda b, h, i: (0, 0))],
        out_specs=pl.BlockSpec((tq, V_DIM), lambda b, h, i: (b * nq + i, h)),
        out_shape=jax.ShapeDtypeStruct(q.shape, BF16),
        scratch_shapes=[pltpu.VMEM((2 * tq, 1), F32), pltpu.VMEM((2 * tq, 1), F32),
                        pltpu.VMEM((2 * tq, V_DIM), F32)],
        compiler_params=_params("parallel", "parallel", "parallel"),
        name="attn_prompt",
    )(q, kb, vb, slopes, lam_params, subln.reshape(1, -1))


def _attn_sample_body(pt_ref, q_ref, *rest, pages_per_step, past, lam_init, n_tok):
    k_refs = rest[:pages_per_step]
    v_refs = rest[pages_per_step:2 * pages_per_step]
    (kn_ref, vn_ref, b0_ref, bn_ref, rowc_ref, lam_ref, sub_ref, o_ref,
     m_ref, l_ref, acc_ref) = rest[2 * pages_per_step:]
    s_idx = pl.program_id(1)
    nt = (((1,), (1,)), ((), ()))

    @pl.when(s_idx == 0)
    def _():
        m_ref[...] = jnp.full(m_ref.shape, NEG, F32)
        l_ref[...] = jnp.zeros(l_ref.shape, F32)
        acc_ref[...] = jnp.zeros(acc_ref.shape, F32)

    q = q_ref[...]
    slope = rowc_ref[:, 0:1]
    slope_tok = rowc_ref[:, 1:2]
    for i in range(pages_per_step):
        page = s_idx * pages_per_step + i
        kp = k_refs[i][...].astype(BF16)
        vp = v_refs[i][...].astype(BF16)
        s = lax.dot_general(q, kp, nt, preferred_element_type=F32) + b0_ref[...]
        offset = slope * (page * PAGE_SIZE - past).astype(F32) - slope_tok
        _online_softmax_step(s, vp, m_ref, l_ref, acc_ref, offset)

    @pl.when(s_idx == pl.num_programs(1) - 1)
    def _():
        s = lax.dot_general(q, kn_ref[...], nt, preferred_element_type=F32) + bn_ref[...]
        _online_softmax_step(s, vn_ref[...], m_ref, l_ref, acc_ref)
        lam = _diff_lambda(lam_ref, lam_init)
        o_ref[...] = _diff_finish(acc_ref, l_ref, N_HEADS * n_tok, lam, sub_ref, lam_init).astype(BF16)


def _attn_sample(q, kb, vb, cache_k, cache_v, page_table, lam_params, subln, lam_init, n_tok, pages_per_step):
    db, n_pages = page_table.shape
    past = n_pages * PAGE_SIZE
    n_pool = cache_k.shape[0]
    rows_q = 2 * N_HEADS * n_tok
    rows_p = PAGE_SIZE * N_HEADS
    slopes = _alibi_slopes()
    q5 = q.reshape(db, n_tok, N_HEADS, 2, HEAD_DIM).transpose(0, 3, 2, 1, 4)
    zq = jnp.zeros_like(q5[:, 0])
    qall = jnp.stack([jnp.concatenate([q5[:, 0], zq], -1), jnp.concatenate([zq, q5[:, 1]], -1)], 1)
    qall = qall.reshape(db, rows_q, V_DIM)
    pad_new = lambda x: jnp.pad(x.reshape(db, n_tok, N_HEADS, V_DIM),
                                ((0, 0), (0, NEW_KEYS_PAD - n_tok), (0, 0), (0, 0))).reshape(db, LANES, V_DIM)
    r = jnp.arange(rows_q)
    r_head, r_tok = (r // n_tok) % N_HEADS, r % n_tok
    r_slope = slopes[r_head]
    c = jnp.arange(rows_p)
    c_key, c_head = c // N_HEADS, c % N_HEADS
    b0 = jnp.where(r_head[:, None] == c_head[None, :], r_slope[:, None] * c_key[None, :].astype(F32), NEG)
    cn = jnp.arange(LANES)
    n_key, n_head = cn // N_HEADS, cn % N_HEADS
    ok = (r_head[:, None] == n_head[None, :]) & (n_key[None, :] <= r_tok[:, None])
    bn = jnp.where(ok, -r_slope[:, None] * (r_tok[:, None] - n_key[None, :]).astype(F32), NEG)
    rowc = jnp.zeros((rows_q, LANES), F32).at[:, 0].set(r_slope).at[:, 1].set(r_slope * r_tok.astype(F32))

    kc = cache_k.reshape(n_pool, rows_p, V_DIM)
    vc = cache_v.reshape(n_pool, rows_p, V_DIM)
    page_spec = lambda i: pl.BlockSpec((None, rows_p, V_DIM),
                                       lambda b, s, pt: (pt[b, s * pages_per_step + i], 0, 0))
    per_b = lambda rows: pl.BlockSpec((None, rows, V_DIM), lambda b, s, pt: (b, 0, 0))
    const = lambda shape: pl.BlockSpec(shape, lambda b, s, pt: (0,) * len(shape))
    body = functools.partial(_attn_sample_body, pages_per_step=pages_per_step, past=past,
                             lam_init=lam_init, n_tok=n_tok)
    out = pl.pallas_call(
        body,
        grid_spec=pltpu.PrefetchScalarGridSpec(
            num_scalar_prefetch=1,
            grid=(db, n_pages // pages_per_step),
            in_specs=([per_b(rows_q)] + [page_spec(i) for i in range(pages_per_step)] * 2
                      + [per_b(LANES), per_b(LANES), const((rows_q, rows_p)), const((rows_q, LANES)),
                         const((rows_q, LANES)), const((4, HEAD_DIM)), const((1, V_DIM))]),
            out_specs=pl.BlockSpec((None, N_HEADS * n_tok, V_DIM), lambda b, s, pt: (b, 0, 0)),
            scratch_shapes=[pltpu.VMEM((rows_q, 1), F32), pltpu.VMEM((rows_q, 1), F32),
                            pltpu.VMEM((rows_q, V_DIM), F32)]),
        out_shape=jax.ShapeDtypeStruct((db, N_HEADS * n_tok, V_DIM), BF16),
        compiler_params=_params("parallel", "arbitrary"),
        name="attn_sample",
    )(page_table, qall, *([kc] * pages_per_step), *([vc] * pages_per_step),
      pad_new(kb), pad_new(vb), b0, bn, rowc, lam_params, subln.reshape(1, -1))
    return out.reshape(db, N_HEADS, n_tok, V_DIM).transpose(0, 2, 1, 3).reshape(db * n_tok, N_HEADS * V_DIM)


def _s5_disc_body(ar_ref, ai_ref, ls_ref, br_ref, bi_ref, abr_ref, abi_ref, bbr_ref, bbi_ref):
    ar, ai = ar_ref[...], ai_ref[...]
    dt = jnp.exp(ls_ref[...])
    mag = jnp.exp(ar * dt)
    ang = ai * dt
    abr = mag * jnp.cos(ang)
    abi = mag * jnp.sin(ang)
    den = ar * ar + ai * ai
    nr = abr - 1.0
    fr = (nr * ar + abi * ai) / den
    fi = (abi * ar - nr * ai) / den
    abr_ref[...] = abr
    abi_ref[...] = abi
    br, bi = br_ref[...], bi_ref[...]
    bbr_ref[...] = fr * br - fi * bi
    bbi_ref[...] = fr * bi + fi * br


def _s5_disc(a_re, a_im, log_step, b_re, b_im):
    g, p = a_re.shape
    rows = g * GROUP
    rep = lambda a: jnp.repeat(a, GROUP, axis=0)
    bt = lambda b: b.transpose(0, 2, 1).reshape(rows, p)
    full = lambda shape: pl.BlockSpec(shape, lambda: (0,) * len(shape))
    abr, abi, bbr, bbi = pl.pallas_call(
        _s5_disc_body,
        in_specs=[full((rows, p)), full((rows, p)), full((rows, 1)), full((rows, p)), full((rows, p))],
        out_specs=[full((rows, p))] * 4,
        out_shape=[jax.ShapeDtypeStruct((rows, p), F32)] * 4,
        name="s5_disc",
    )(rep(a_re), rep(a_im), rep(log_step.reshape(g, 1)), bt(b_re), bt(b_im))
    pick = lambda a: a.reshape(g, GROUP, p)[:, 0]
    return pick(abr), pick(abi), bbr.reshape(g, GROUP, p), bbi.reshape(g, GROUP, p)


def _block_diag_in(bb):
    gpc = CH_CHUNK // GROUP
    x = bb.reshape(-1, gpc, GROUP, STATE)
    eye = jnp.eye(gpc, dtype=bb.dtype)
    return jnp.einsum("ngcp,gh->ngchp", x, eye).reshape(-1, CH_CHUNK, ST_CHUNK)


def _block_diag_out(cc):
    gpc = CH_CHUNK // GROUP
    x = cc.reshape(-1, gpc, GROUP, STATE)
    eye = jnp.eye(gpc, dtype=cc.dtype)
    return jnp.einsum("ngcp,gh->ngphc", x, eye).reshape(-1, ST_CHUNK, CH_CHUNK)


def _s5_body(u_ref, g_ref, ar_ref, ai_ref, wb_ref, wcr_ref, wci_ref, d_ref, h0r_ref, h0i_ref,
             y_ref, sr_ref, si_ref, hn_ref, x_ref, *, batch, steps, slab):
    @pl.when(pl.program_id(0) == 0)
    def _():
        sr_ref[...] = h0r_ref[...]
        si_ref[...] = h0i_ref[...]

    hn_ref[...] = _rms(u_ref[...], g_ref[...])
    n_chunks = D_MODEL // CH_CHUNK
    for c in range(n_chunks):
        ch = slice(c * CH_CHUNK, (c + 1) * CH_CHUNK)
        u_c = hn_ref[:, ch]
        x_ref[...] = jnp.dot(u_c.astype(BF16), wb_ref[c], preferred_element_type=F32)
        for s0 in range(0, ST_CHUNK, slab):
            lanes = slice(c * ST_CHUNK + s0, c * ST_CHUNK + s0 + slab)
            a_r = jnp.broadcast_to(ar_ref[:, lanes], (batch, slab))
            a_i = jnp.broadcast_to(ai_ref[:, lanes], (batch, slab))
            re = slice(s0, s0 + slab)
            im = slice(ST_CHUNK + s0, ST_CHUNK + s0 + slab)

            def step(t, carry):
                h_r, h_i = carry
                rows = pl.ds(pl.multiple_of(t * batch, batch), batch)
                n_r = a_r * h_r - a_i * h_i + x_ref[rows, re]
                n_i = a_r * h_i + a_i * h_r + x_ref[rows, im]
                x_ref[rows, re] = n_r
                x_ref[rows, im] = n_i
                return n_r, n_i

            h_r, h_i = lax.fori_loop(0, steps, step, (sr_ref[:, lanes], si_ref[:, lanes]))
            sr_ref[:, lanes] = h_r
            si_ref[:, lanes] = h_i
        y = (jnp.dot(x_ref[:, :ST_CHUNK].astype(BF16), wcr_ref[c], preferred_element_type=F32)
             - jnp.dot(x_ref[:, ST_CHUNK:].astype(BF16), wci_ref[c], preferred_element_type=F32))
        y = y + d_ref[:, ch] * u_c
        y_ref[:, ch] = jax.nn.gelu(y).astype(BF16)


def _s5(u_tm, gain, abr, abi, wb, wcr, wci, d_skip, h0r, h0i, batch, steps):
    rows_total = u_tm.shape[0]
    rows = steps * batch
    slab = min(ST_CHUNK, SUBLANES * LANES * 4 // batch)
    body = functools.partial(_s5_body, batch=batch, steps=steps, slab=slab)
    state = jax.ShapeDtypeStruct((batch, N_STATE), F32)
    return pl.pallas_call(
        body,
        grid=(rows_total // rows,),
        in_specs=[pl.BlockSpec((rows, D_MODEL), lambda k: (k, 0)),
                  _resident((1, D_MODEL)), _resident((1, N_STATE)), _resident((1, N_STATE)),
                  _resident(wb.shape), _resident(wcr.shape), _resident(wci.shape),
                  _resident((1, D_MODEL)), _resident((batch, N_STATE)), _resident((batch, N_STATE))],
        out_specs=[pl.BlockSpec((rows, D_MODEL), lambda k: (k, 0)),
                   pl.BlockSpec((batch, N_STATE), lambda k: (0, 0)),
                   pl.BlockSpec((batch, N_STATE), lambda k: (0, 0))],
        out_shape=[jax.ShapeDtypeStruct((rows_total, D_MODEL), BF16), state, state],
        scratch_shapes=[pltpu.VMEM((rows, D_MODEL), F32), pltpu.VMEM((rows, 2 * ST_CHUNK), F32)],
        compiler_params=_params("arbitrary"),
        name="s5",
    )(u_tm, gain.reshape(1, -1), abr.reshape(1, -1), abi.reshape(1, -1), wb, wcr, wci,
      d_skip.reshape(1, -1), h0r, h0i)


def _tail_body(h_ref, a_ref, p_ref, *rest, mixer, hid_chunks):
    n_mix = 1 if mixer == "attn" else 2
    mix_w = rest[:n_mix]
    (nf_ref, wg_ref, wu_ref, wd_ref, np_ref, wpg_ref, wpp_ref, o_ref) = rest[n_mix:]
    a = a_ref[...]
    if mixer == "attn":
        mix = jnp.dot(a, mix_w[0][...], preferred_element_type=F32)
    else:
        mix = (jnp.dot(a, mix_w[0][...], preferred_element_type=F32)
               * jax.nn.sigmoid(jnp.dot(a, mix_w[1][...], preferred_element_type=F32)))
    h = h_ref[...] + mix
    x = _rms(h, nf_ref[...]).astype(BF16)
    ffn = None
    for lo, hi in hid_chunks:
        act = (jax.nn.silu(jnp.dot(x, wg_ref[:, lo:hi], preferred_element_type=F32))
               * jnp.dot(x, wu_ref[:, lo:hi], preferred_element_type=F32)).astype(BF16)
        part = jnp.dot(act, wd_ref[lo:hi, :], preferred_element_type=F32)
        ffn = part if ffn is None else ffn + part
    h = h + ffn
    gate = jax.nn.sigmoid(jnp.dot(_rms(h, np_ref[...]).astype(BF16), wpg_ref[...], preferred_element_type=F32))
    o_ref[...] = h + gate * jnp.dot(p_ref[...].astype(BF16), wpp_ref[...], preferred_element_type=F32)


def _tail(h, a, p, mix_w, norm_ffn, w_gate, w_up, w_down, norm_ple, w_ple_gate, w_ple_proj,
          *, mixer, batch, tm, in_layout="bm", out_layout="bm"):
    m = h.shape[0]
    n_t = m // batch // tm
    hidden = w_gate.shape[1]
    n_chunk = 2 if hidden % (2 * LANES) == 0 else 1
    step = hidden // n_chunk
    hid_chunks = tuple((i * step, (i + 1) * step) for i in range(n_chunk))
    body = functools.partial(_tail_body, mixer=mixer, hid_chunks=hid_chunks)
    ple = p.shape[1]
    out = pl.pallas_call(
        body,
        grid=(m // tm,),
        in_specs=([_row_spec(in_layout, tm, D_MODEL, n_t), _row_spec(in_layout, tm, D_MODEL, n_t),
                   _row_spec("bm", tm, ple, n_t)]
                  + [_resident(w.shape) for w in mix_w]
                  + [_resident((1, D_MODEL)), _resident(w_gate.shape), _resident(w_up.shape),
                     _resident(w_down.shape), _resident((1, D_MODEL)), _resident(w_ple_gate.shape),
                     _resident(w_ple_proj.shape)]),
        out_specs=_row_spec(out_layout, tm, D_MODEL, n_t),
        out_shape=jax.ShapeDtypeStruct(_row_view(h, out_layout, batch).shape, F32),
        compiler_params=_params("parallel"),
        name="tail_" + mixer,
    )(_row_view(h, in_layout, batch), _row_view(a, in_layout, batch), p, *mix_w,
      norm_ffn.reshape(1, -1), w_gate, w_up, w_down, norm_ple.reshape(1, -1), w_ple_gate, w_ple_proj)
    return out.reshape(m, D_MODEL)


def _trunk(x, p, is_sample, wts, cache_k, cache_v, state_re, state_im, page_table):
    batch, seq, _ = x.shape
    m = batch * seq
    tm = min(512, seq) if not is_sample else m
    tail_batch = batch if not is_sample else 1
    h = x.reshape(m, D_MODEL)
    h_layout = "bm"
    k_rows, v_rows, s_re, s_im = [], [], [], []
    for i in range(DEPTH):
        j = i // 2
        p_i = p[i].reshape(m, -1)
        tail_w = (wts["norm_ffn"][i], wts["w_ffn_gate"][i], wts["w_ffn_up"][i], wts["w_ffn_down"][i],
                  wts["norm_ple"][i], wts["w_ple_gate"][i], wts["w_ple_proj"][i])
        next_is_ssm = (i + 1 < DEPTH) and ((i + 1) % 2 == 1) and not is_sample
        out_layout = "tm" if next_is_ssm else "bm"
        if i % 2 == 0:
            lam0 = _lambda_init(i)
            lam_params = jnp.stack([wts["lam_q1"][j], wts["lam_k1"][j], wts["lam_q2"][j], wts["lam_k2"][j]])
            q, k, v, kb, vb = _qkv(h, wts["norm_mix"][i], wts["w_qkv"][j], wts["q_norm"][j], wts["k_norm"][j],
                                   tm=min(512, m))
            if is_sample:
                o = _attn_sample(q, kb, vb, cache_k[j], cache_v[j], page_table, lam_params, wts["subln"][j],
                                 lam0, seq, pages_per_step=8)
            else:
                o = _attn_prompt(q, kb, vb, lam_params, wts["subln"][j], lam0, batch, seq, tq=256, tk=256)
            k_rows.append(k.reshape(batch, seq, N_HEADS, 2 * HEAD_DIM))
            v_rows.append(v.reshape(batch, seq, N_HEADS, V_DIM))
            h = _tail(h, o, p_i, (wts["w_o"][j],), *tail_w, mixer="attn", batch=tail_batch, tm=tm,
                      in_layout="bm", out_layout=out_layout)
            h_layout = out_layout
        else:
            abr, abi, bbr, bbi = _s5_disc(wts["ssm_a_re"][j], wts["ssm_a_im"][j], wts["ssm_log_step"][j],
                                          wts["ssm_b_re"][j], wts["ssm_b_im"][j])
            wb = jnp.concatenate([_block_diag_in(bbr), _block_diag_in(bbi)], axis=-1).astype(BF16)
            wcr = _block_diag_out(wts["ssm_c_re"][j]).astype(BF16)
            wci = _block_diag_out(wts["ssm_c_im"][j]).astype(BF16)
            if is_sample:
                u_tm = h.reshape(batch, seq, D_MODEL).transpose(1, 0, 2).reshape(m, D_MODEL)
                h0r = state_re[j].reshape(batch, N_STATE)
                h0i = state_im[j].reshape(batch, N_STATE)
                steps = seq
            else:
                assert h_layout == "tm"
                u_tm = h
                h0r = h0i = jnp.zeros((batch, N_STATE), F32)
                steps = 128
            g, hr, hi = _s5(u_tm, wts["norm_mix"][i], abr, abi, wb, wcr, wci, wts["ssm_d"][j], h0r, h0i,
                            batch, steps)
            if is_sample:
                g = g.reshape(seq, batch, D_MODEL).transpose(1, 0, 2).reshape(m, D_MODEL)
            s_re.append(hr.reshape(batch, N_GROUPS, STATE))
            s_im.append(hi.reshape(batch, N_GROUPS, STATE))
            h = _tail(h, g, p_i, (wts["w_glu_a"][j], wts["w_glu_b"][j]), *tail_w, mixer="ssm",
                      batch=tail_batch, tm=tm, in_layout=h_layout, out_layout=out_layout)
            h_layout = out_layout
    return (h.reshape(batch, seq, D_MODEL), jnp.stack(k_rows), jnp.stack(v_rows),
            jnp.stack(s_re), jnp.stack(s_im))


def kernel(x_prompt, x_sample, cache_k, cache_v, state_ssm_re, state_ssm_im, page_table, p_prompt, p_sample, norm_mix, norm_ffn, norm_ple, w_qkv, q_norm, k_norm, lam_q1, lam_k1, lam_q2, lam_k2, subln, w_o, ssm_a_re, ssm_a_im, ssm_log_step, ssm_b_re, ssm_b_im, ssm_c_re, ssm_c_im, ssm_d, w_glu_a, w_glu_b, w_ffn_gate, w_ffn_up, w_ffn_down, w_ple_proj, w_ple_gate):
    bf = lambda w: w.astype(BF16)
    wts = dict(norm_mix=norm_mix, norm_ffn=norm_ffn, norm_ple=norm_ple, w_qkv=bf(w_qkv), q_norm=q_norm,
               k_norm=k_norm, lam_q1=lam_q1, lam_k1=lam_k1, lam_q2=lam_q2, lam_k2=lam_k2, subln=subln,
               w_o=bf(w_o), ssm_a_re=ssm_a_re, ssm_a_im=ssm_a_im, ssm_log_step=ssm_log_step,
               ssm_b_re=ssm_b_re, ssm_b_im=ssm_b_im, ssm_c_re=ssm_c_re, ssm_c_im=ssm_c_im, ssm_d=ssm_d,
               w_glu_a=bf(w_glu_a), w_glu_b=bf(w_glu_b), w_ffn_gate=bf(w_ffn_gate), w_ffn_up=bf(w_ffn_up),
               w_ffn_down=bf(w_ffn_down), w_ple_proj=bf(w_ple_proj), w_ple_gate=bf(w_ple_gate))
    y_p, k_p, v_p, sr_p, si_p = _trunk(x_prompt, p_prompt, False, wts, None, None, None, None, None)
    y_s, k_s, v_s, sr_s, si_s = _trunk(x_sample, p_sample, True, wts, cache_k, cache_v,
                                       state_ssm_re, state_ssm_im, page_table)
    return (y_p, y_s, k_p, v_p, k_s, v_s, sr_p, si_p, sr_s, si_s)
```

```python
import functools
import math

import jax
import jax.numpy as jnp
from jax import lax
from jax.experimental import pallas as pl
from jax.experimental.pallas import tpu as pltpu

F32 = jnp.float32
BF16 = jnp.bfloat16

D_MODEL = 1024
N_HEADS = 8
HEAD_DIM = 64
V_DIM = 2 * HEAD_DIM
QK_COLS = N_HEADS * 2 * HEAD_DIM
GROUP = 16
STATE = 64
N_GROUPS = D_MODEL // GROUP
N_STATE = N_GROUPS * STATE
PAGE_SIZE = 128
DEPTH = 4
N_ATTN_LAYERS = 2
EPS = 1e-6
NEG = -1e30
LOG2E = math.log2(math.e)

LANES = 128
SUBLANES = 8
MXU_DIM = 256
VMEM_LIMIT = 56 * 1024 * 1024

CH_CHUNK = MXU_DIM
ST_CHUNK = CH_CHUNK // GROUP * STATE
NEW_KEYS_PAD = LANES // N_HEADS
VX_COLS = 2 * V_DIM


def _lambda_init(layer):
    return 0.8 - 0.6 * math.exp(-0.3 * layer)


def _alibi_slopes():
    return 2.0 ** (-8.0 * jnp.arange(1, N_HEADS + 1, dtype=F32) / N_HEADS)


def _rms(x, gain):
    return x * lax.rsqrt(jnp.mean(x * x, axis=-1, keepdims=True) + EPS) * gain


def _resident(shape):
    nd = len(shape)
    return pl.BlockSpec(shape, lambda *_: (0,) * nd, pipeline_mode=pl.Buffered(1))


def _params(*sem):
    return pltpu.CompilerParams(dimension_semantics=sem, vmem_limit_bytes=VMEM_LIMIT)


def _row_view(x, layout, batch):
    rows, cols = x.shape
    return x if layout == "bm" else x.reshape(rows // batch, batch * cols)


def _row_spec(layout, tm, cols, n_t):
    if layout == "bm":
        return pl.BlockSpec((tm, cols), lambda i: (i, 0))
    return pl.BlockSpec((tm, cols), lambda i: (i % n_t, i // n_t))


def _qkv_body(h_ref, g_ref, w_ref, qn_ref, kn_ref, e_ref, *rest):
    q_ref, kb_ref, vx_ref, k_ref, v_ref = rest[-5:]
    hn = _rms(h_ref[...], g_ref[...]).astype(BF16)
    qkv = jnp.dot(hn, w_ref[...], preferred_element_type=F32)
    tm = qkv.shape[0]

    def head_norm(z, gain):
        sq = (z * z).astype(BF16)
        parts = [jnp.dot(sq[:, j * MXU_DIM:(j + 1) * MXU_DIM], e_ref[...], preferred_element_type=F32)
                 for j in range(QK_COLS // MXU_DIM)]
        ss = jnp.concatenate(parts, axis=1)
        return z * lax.rsqrt(ss * (1.0 / HEAD_DIM) + EPS) * gain

    q = head_norm(qkv[:, :QK_COLS], qn_ref[...]) * (HEAD_DIM ** -0.5 * LOG2E)
    k = head_norm(qkv[:, QK_COLS:2 * QK_COLS], kn_ref[...])
    v = qkv[:, 2 * QK_COLS:]
    q_ref[...] = q.astype(BF16)
    kb_ref[...] = k.astype(BF16)
    vb = v.astype(BF16)
    ones = jnp.ones((tm, V_DIM), BF16)
    for h in range(N_HEADS):
        head = slice(h * V_DIM, (h + 1) * V_DIM)
        vx_ref[:, h * VX_COLS:h * VX_COLS + V_DIM] = vb[:, head]
        vx_ref[:, h * VX_COLS + V_DIM:(h + 1) * VX_COLS] = ones
        k_ref[pl.ds(h, tm, stride=N_HEADS), :] = k[:, head]
        v_ref[pl.ds(h, tm, stride=N_HEADS), :] = v[:, head]


def _qkv(h, gain, w, qn, kn, kv_prev, layer, tm):
    m = h.shape[0]
    n_blk = m // tm
    ones_bd = (jnp.arange(MXU_DIM)[:, None] // HEAD_DIM == jnp.arange(MXU_DIM)[None, :] // HEAD_DIM).astype(BF16)
    row = lambda c: pl.BlockSpec((tm, c), lambda i: (i, 0))
    kv_spec = pl.BlockSpec((tm * N_HEADS, V_DIM), lambda i: (layer * n_blk + i, 0))
    kv_shape = jax.ShapeDtypeStruct((N_ATTN_LAYERS * m * N_HEADS, V_DIM), F32)
    aliased = [] if kv_prev is None else list(kv_prev)
    n_in = 6
    return pl.pallas_call(
        _qkv_body,
        grid=(n_blk,),
        in_specs=[row(D_MODEL), _resident((1, D_MODEL)), _resident(w.shape),
                  _resident((1, QK_COLS)), _resident((1, QK_COLS)), _resident((MXU_DIM, MXU_DIM))]
                 + [pl.BlockSpec(memory_space=pl.ANY)] * len(aliased),
        out_specs=[row(QK_COLS), row(QK_COLS), row(N_HEADS * VX_COLS), kv_spec, kv_spec],
        out_shape=[jax.ShapeDtypeStruct((m, QK_COLS), BF16),
                   jax.ShapeDtypeStruct((m, QK_COLS), BF16),
                   jax.ShapeDtypeStruct((m, N_HEADS * VX_COLS), BF16),
                   kv_shape, kv_shape],
        input_output_aliases={n_in + a: 3 + a for a in range(len(aliased))},
        compiler_params=_params("parallel"),
        name="qkv",
    )(h, gain.reshape(1, -1), w, jnp.tile(qn, QK_COLS // HEAD_DIM).reshape(1, -1),
      jnp.tile(kn, QK_COLS // HEAD_DIM).reshape(1, -1), ones_bd, *aliased)


def _diff_lambda(lam_ref, lam_init):
    l = lam_ref[...]
    a = jnp.sum(l[0:1] * l[1:2], axis=-1, keepdims=True)
    b = jnp.sum(l[2:3] * l[3:4], axis=-1, keepdims=True)
    return jnp.exp(a) - jnp.exp(b) + lam_init


def _diff_finish(o1, o2, lam, sub_ref, lam_init):
    return _rms(o1 - lam * o2, sub_ref[...]) * (1.0 - lam_init)


def _attn_prompt_body(q_ref, k_ref, vx_ref, slope_ref, lam_ref, sub_ref, o_ref,
                      q2_ref, s_ref, m_ref, acc_ref, *, tq, heads, lam_init):
    n_q = q_ref.shape[0] // tq
    lam = _diff_lambda(lam_ref, lam_init)
    nt = (((1,), (1,)), ((), ()))
    key_off = lax.broadcasted_iota(jnp.int32, (1, tq), 1).astype(F32)
    slopes = [slope_ref[hh][:, 0:1] * LOG2E for hh in range(heads)]
    qk = lambda hh: slice(hh * V_DIM, (hh + 1) * V_DIM)
    vxc = lambda hh: slice(hh * VX_COLS, (hh + 1) * VX_COLS)

    def q_block(qi, carry):
        q0 = pl.multiple_of(qi * tq, tq)
        for hh in range(heads):
            q = q_ref[pl.ds(q0, tq), qk(hh)]
            lane = lax.broadcasted_iota(jnp.int32, q.shape, 1)
            zero = jnp.zeros_like(q)
            q2_ref[hh, 0:tq, :] = jnp.where(lane < HEAD_DIM, q, zero)
            q2_ref[hh, tq:2 * tq, :] = jnp.where(lane >= HEAD_DIM, q, zero)
        m_ref[...] = jnp.full(m_ref.shape, NEG, F32)

        def scores(j, diagonal):
            k0 = pl.multiple_of(j * tq, tq)
            for hh in range(heads):
                s = lax.dot_general(q2_ref[hh], k_ref[pl.ds(k0, tq), qk(hh)], nt, preferred_element_type=F32)
                s = s + slopes[hh] * (key_off + (j * tq).astype(F32))
                if diagonal:
                    future = (lax.broadcasted_iota(jnp.int32, (tq, tq), 1)
                              > lax.broadcasted_iota(jnp.int32, (tq, tq), 0))
                    s = jnp.where(jnp.concatenate([future, future], axis=0), NEG, s)
                s_ref[hh, j] = s
                part = s[:, 0:LANES]
                for c in range(1, tq // LANES):
                    part = jnp.maximum(part, s[:, c * LANES:(c + 1) * LANES])
                m_ref[hh] = jnp.maximum(m_ref[hh], part)

        def full_chunk(j, c):
            scores(j, False)
            return c

        lax.fori_loop(0, qi, full_chunk, 0)
        scores(qi, True)
        for hh in range(heads):
            m_ref[hh] = jnp.broadcast_to(jnp.max(m_ref[hh], axis=-1, keepdims=True), (2 * tq, LANES))

        def weighted(j, k0):
            out = []
            for hh in range(heads):
                m_row = m_ref[hh]
                p = jnp.exp2(s_ref[hh, j] - jnp.concatenate([m_row] * (tq // LANES), axis=1)).astype(BF16)
                out.append(jnp.dot(p, vx_ref[pl.ds(k0, tq), vxc(hh)], preferred_element_type=F32))
            return out

        for hh, w in enumerate(weighted(0, 0)):
            acc_ref[hh] = w

        def more(j, c):
            for hh, w in enumerate(weighted(j, pl.multiple_of(j * tq, tq))):
                acc_ref[hh] += w
            return c

        lax.fori_loop(1, qi + 1, more, 0)
        for hh in range(heads):
            o1 = acc_ref[hh, 0:tq, 0:V_DIM] / acc_ref[hh, 0:tq, V_DIM:VX_COLS]
            o2 = acc_ref[hh, tq:2 * tq, 0:V_DIM] / acc_ref[hh, tq:2 * tq, V_DIM:VX_COLS]
            o_ref[pl.ds(q0, tq), qk(hh)] = _diff_finish(o1, o2, lam, sub_ref, lam_init).astype(BF16)
        return carry

    lax.fori_loop(0, n_q, q_block, 0)


def _attn_prompt(q, kb, vx, lam_params, subln, lam_init, batch, seq, tq, heads):
    slopes = jnp.broadcast_to(_alibi_slopes()[:, None, None], (N_HEADS, 1, LANES))
    body = functools.partial(_attn_prompt_body, tq=tq, heads=heads, lam_init=lam_init)
    per_head = lambda cols: pl.BlockSpec((seq, heads * cols), lambda b, h: (b, h))
    return pl.pallas_call(
        body,
        grid=(batch, N_HEADS // heads),
        in_specs=[per_head(V_DIM), per_head(V_DIM), per_head(VX_COLS),
                  pl.BlockSpec((heads, 1, LANES), lambda b, h: (h, 0, 0)),
                  pl.BlockSpec((4, HEAD_DIM), lambda b, h: (0, 0)),
                  pl.BlockSpec((1, V_DIM), lambda b, h: (0, 0))],
        out_specs=per_head(V_DIM),
        out_shape=jax.ShapeDtypeStruct(q.shape, BF16),
        scratch_shapes=[pltpu.VMEM((heads, 2 * tq, V_DIM), BF16),
                        pltpu.VMEM((heads, seq // tq, 2 * tq, tq), F32),
                        pltpu.VMEM((heads, 2 * tq, LANES), F32),
                        pltpu.VMEM((heads, 2 * tq, VX_COLS), F32)],
        compiler_params=_params("parallel", "parallel"),
        name="attn_prompt",
    )(q, kb, vx, slopes, lam_params, subln.reshape(1, -1))


def _softmax_update(s, pv, m_ref, l_ref, acc_ref, offset=None):
    m_prev = m_ref[...]
    m_cur = jnp.max(s, axis=-1, keepdims=True)
    if offset is not None:
        m_cur = m_cur + offset
    m_new = jnp.maximum(m_prev, m_cur)
    shift = m_new if offset is None else m_new - offset
    p = jnp.exp2(s - shift)
    alpha = jnp.exp2(m_prev - m_new)
    l_ref[...] = alpha * l_ref[...] + jnp.sum(p, axis=-1, keepdims=True)
    acc_ref[...] = alpha * acc_ref[...] + pv(p.astype(BF16))
    m_ref[...] = m_new


def _attn_sample_body(pt_ref, q_ref, *rest, pages_per_step, past, lam_init, n_tok):
    k_refs = rest[:pages_per_step]
    v_refs = rest[pages_per_step:2 * pages_per_step]
    (kn_ref, vn_ref, b0_ref, bn_ref, rowc_ref, lam_ref, sub_ref, o_ref,
     m_ref, l_ref, acc_ref) = rest[2 * pages_per_step:]
    s_idx = pl.program_id(1)
    nt = (((1,), (1,)), ((), ()))
    rows_p = k_refs[0].shape[0]

    @pl.when(s_idx == 0)
    def _():
        m_ref[...] = jnp.full(m_ref.shape, NEG, F32)
        l_ref[...] = jnp.zeros(l_ref.shape, F32)
        acc_ref[...] = jnp.zeros(acc_ref.shape, F32)

    q = q_ref[...]
    first_key = s_idx * (pages_per_step * PAGE_SIZE) - past
    offset = rowc_ref[:, 0:1] * first_key.astype(F32) - rowc_ref[:, 1:2]
    s = jnp.concatenate(
        [lax.dot_general(q, k_refs[i][...].astype(BF16), nt, preferred_element_type=F32)
         for i in range(pages_per_step)], axis=1) + b0_ref[...]

    def pv(p):
        out = None
        for i in range(pages_per_step):
            part = jnp.dot(p[:, i * rows_p:(i + 1) * rows_p], v_refs[i][...].astype(BF16),
                           preferred_element_type=F32)
            out = part if out is None else out + part
        return out

    _softmax_update(s, pv, m_ref, l_ref, acc_ref, offset)

    @pl.when(s_idx == pl.num_programs(1) - 1)
    def _():
        s_new = lax.dot_general(q, kn_ref[...], nt, preferred_element_type=F32) + bn_ref[...]
        _softmax_update(s_new, lambda p: jnp.dot(p, vn_ref[...], preferred_element_type=F32),
                        m_ref, l_ref, acc_ref)
        lam = _diff_lambda(lam_ref, lam_init)
        half = N_HEADS * n_tok
        o1 = acc_ref[0:half, :] / l_ref[0:half, :]
        o2 = acc_ref[half:2 * half, :] / l_ref[half:2 * half, :]
        o_ref[...] = _diff_finish(o1, o2, lam, sub_ref, lam_init).astype(BF16)


def _attn_sample(q, kb, vb, cache_k, cache_v, layer, page_table, lam_params, subln, lam_init, n_tok,
                 pages_per_step):
    db, n_pages = page_table.shape
    past = n_pages * PAGE_SIZE
    n_pool = cache_k.shape[1]
    rows_q = 2 * N_HEADS * n_tok
    rows_p = PAGE_SIZE * N_HEADS
    slopes = _alibi_slopes() * LOG2E
    q5 = q.reshape(db, n_tok, N_HEADS, 2, HEAD_DIM).transpose(0, 3, 2, 1, 4)
    zq = jnp.zeros_like(q5[:, 0])
    qall = jnp.stack([jnp.concatenate([q5[:, 0], zq], -1), jnp.concatenate([zq, q5[:, 1]], -1)], 1)
    qall = qall.reshape(db, rows_q, V_DIM)
    pad_new = lambda x: jnp.pad(x.reshape(db, n_tok, N_HEADS, V_DIM),
                                ((0, 0), (0, NEW_KEYS_PAD - n_tok), (0, 0), (0, 0))).reshape(db, LANES, V_DIM)
    r = jnp.arange(rows_q)
    r_head, r_tok = (r // n_tok) % N_HEADS, r % n_tok
    r_slope = slopes[r_head]
    c = jnp.arange(pages_per_step * rows_p)
    c_key, c_head = c // N_HEADS, c % N_HEADS
    b0 = jnp.where(r_head[:, None] == c_head[None, :], r_slope[:, None] * c_key[None, :].astype(F32), NEG)
    cn = jnp.arange(LANES)
    n_key, n_head = cn // N_HEADS, cn % N_HEADS
    ok = (r_head[:, None] == n_head[None, :]) & (n_key[None, :] <= r_tok[:, None])
    bn = jnp.where(ok, -r_slope[:, None] * (r_tok[:, None] - n_key[None, :]).astype(F32), NEG)
    rowc = jnp.zeros((rows_q, LANES), F32).at[:, 0].set(r_slope).at[:, 1].set(r_slope * r_tok.astype(F32))

    kc = cache_k.reshape(-1, rows_p, V_DIM)
    vc = cache_v.reshape(-1, rows_p, V_DIM)
    page_spec = lambda i: pl.BlockSpec(
        (None, rows_p, V_DIM), lambda b, s, pt: (layer * n_pool + pt[b, s * pages_per_step + i], 0, 0))
    per_b = lambda rows: pl.BlockSpec((None, rows, V_DIM), lambda b, s, pt: (b, 0, 0))
    const = lambda shape: pl.BlockSpec(shape, lambda b, s, pt: (0,) * len(shape))
    body = functools.partial(_attn_sample_body, pages_per_step=pages_per_step, past=past,
                             lam_init=lam_init, n_tok=n_tok)
    out = pl.pallas_call(
        body,
        grid_spec=pltpu.PrefetchScalarGridSpec(
            num_scalar_prefetch=1,
            grid=(db, n_pages // pages_per_step),
            in_specs=([per_b(rows_q)] + [page_spec(i) for i in range(pages_per_step)] * 2
                      + [per_b(LANES), per_b(LANES), const(b0.shape), const((rows_q, LANES)),
                         const((rows_q, LANES)), const((4, HEAD_DIM)), const((1, V_DIM))]),
            out_specs=pl.BlockSpec((None, N_HEADS * n_tok, V_DIM), lambda b, s, pt: (b, 0, 0)),
            scratch_shapes=[pltpu.VMEM((rows_q, 1), F32), pltpu.VMEM((rows_q, 1), F32),
                            pltpu.VMEM((rows_q, V_DIM), F32)]),
        out_shape=jax.ShapeDtypeStruct((db, N_HEADS * n_tok, V_DIM), BF16),
        compiler_params=_params("parallel", "arbitrary"),
        name="attn_sample",
    )(page_table, qall, *([kc] * pages_per_step), *([vc] * pages_per_step),
      pad_new(kb), pad_new(vb), b0, bn, rowc, lam_params, subln.reshape(1, -1))
    return out.reshape(db, N_HEADS, n_tok, V_DIM).transpose(0, 2, 1, 3).reshape(db * n_tok, N_HEADS * V_DIM)


def _s5_disc_body(ar_ref, ai_ref, ls_ref, br_ref, bi_ref, abr_ref, abi_ref, bbr_ref, bbi_ref):
    ar, ai = ar_ref[...], ai_ref[...]
    dt = jnp.exp(ls_ref[...])
    mag = jnp.exp(ar * dt)
    ang = ai * dt
    abr = mag * jnp.cos(ang)
    abi = mag * jnp.sin(ang)
    den = ar * ar + ai * ai
    nr = abr - 1.0
    fr = (nr * ar + abi * ai) / den
    fi = (abi * ar - nr * ai) / den
    abr_ref[...] = abr
    abi_ref[...] = abi
    br, bi = br_ref[...], bi_ref[...]
    bbr_ref[...] = fr * br - fi * bi
    bbi_ref[...] = fr * bi + fi * br


def _s5_disc(a_re, a_im, log_step, b_re, b_im):
    g, p = a_re.shape
    rows = g * GROUP
    rep = lambda a: jnp.repeat(a, GROUP, axis=0)
    bt = lambda b: b.transpose(0, 2, 1).reshape(rows, p)
    full = lambda shape: pl.BlockSpec(shape, lambda: (0,) * len(shape))
    abr, abi, bbr, bbi = pl.pallas_call(
        _s5_disc_body,
        in_specs=[full((rows, p)), full((rows, p)), full((rows, 1)), full((rows, p)), full((rows, p))],
        out_specs=[full((rows, p))] * 4,
        out_shape=[jax.ShapeDtypeStruct((rows, p), F32)] * 4,
        name="s5_disc",
    )(rep(a_re), rep(a_im), rep(log_step.reshape(g, 1)), bt(b_re), bt(b_im))
    pick = lambda a: a.reshape(g, GROUP, p)[:, 0]
    return pick(abr), pick(abi), bbr.reshape(g, GROUP, p), bbi.reshape(g, GROUP, p)


def _block_diag_in(bb):
    gpc = CH_CHUNK // GROUP
    x = bb.reshape(-1, gpc, GROUP, STATE)
    eye = jnp.eye(gpc, dtype=bb.dtype)
    return jnp.einsum("ngcp,gh->ngchp", x, eye).reshape(-1, CH_CHUNK, ST_CHUNK)


def _block_diag_out(cc):
    gpc = CH_CHUNK // GROUP
    x = cc.reshape(-1, gpc, GROUP, STATE)
    eye = jnp.eye(gpc, dtype=cc.dtype)
    return jnp.einsum("ngcp,gh->ngphc", x, eye).reshape(-1, ST_CHUNK, CH_CHUNK)


def _s5_body(u_ref, g_ref, ar_ref, ai_ref, wb_ref, wcr_ref, wci_ref, d_ref, h0r_ref, h0i_ref,
             y_ref, sr_ref, si_ref, hn_ref, x_ref, *, batch, steps, slab):
    @pl.when(pl.program_id(0) == 0)
    def _():
        sr_ref[...] = h0r_ref[...]
        si_ref[...] = h0i_ref[...]

    hn_ref[...] = _rms(u_ref[...], g_ref[...])
    n_chunks = D_MODEL // CH_CHUNK
    for c in range(n_chunks):
        ch = slice(c * CH_CHUNK, (c + 1) * CH_CHUNK)
        u_c = hn_ref[:, ch]
        x_ref[...] = jnp.dot(u_c.astype(BF16), wb_ref[c], preferred_element_type=F32)
        for s0 in range(0, ST_CHUNK, slab):
            lanes = slice(c * ST_CHUNK + s0, c * ST_CHUNK + s0 + slab)
            re = slice(s0, s0 + slab)
            im = slice(ST_CHUNK + s0, ST_CHUNK + s0 + slab)

            def step(t, carry):
                h_r, h_i = carry
                a_r = ar_ref[:, lanes]
                a_i = ai_ref[:, lanes]
                rows = pl.ds(pl.multiple_of(t * batch, batch), batch)
                n_r = a_r * h_r - a_i * h_i + x_ref[rows, re]
                n_i = a_r * h_i + a_i * h_r + x_ref[rows, im]
                x_ref[rows, re] = n_r
                x_ref[rows, im] = n_i
                return n_r, n_i

            h_r, h_i = lax.fori_loop(0, steps, step, (sr_ref[:, lanes], si_ref[:, lanes]))
            sr_ref[:, lanes] = h_r
            si_ref[:, lanes] = h_i
        y = (jnp.dot(x_ref[:, :ST_CHUNK].astype(BF16), wcr_ref[c], preferred_element_type=F32)
             - jnp.dot(x_ref[:, ST_CHUNK:].astype(BF16), wci_ref[c], preferred_element_type=F32))
        y = y + d_ref[:, ch] * u_c
        y_ref[:, ch] = jax.nn.gelu(y).astype(BF16)


def _s5(u_tm, gain, abr, abi, wb, wcr, wci, d_skip, h0r, h0i, batch, steps):
    rows_total = u_tm.shape[0]
    rows = steps * batch
    slab = min(ST_CHUNK, SUBLANES * LANES * 4 // batch)
    body = functools.partial(_s5_body, batch=batch, steps=steps, slab=slab)
    state = jax.ShapeDtypeStruct((batch, N_STATE), F32)
    return pl.pallas_call(
        body,
        grid=(rows_total // rows,),
        in_specs=[pl.BlockSpec((rows, D_MODEL), lambda k: (k, 0)),
                  _resident((1, D_MODEL)), _resident((batch, N_STATE)), _resident((batch, N_STATE)),
                  _resident(wb.shape), _resident(wcr.shape), _resident(wci.shape),
                  _resident((1, D_MODEL)), _resident((batch, N_STATE)), _resident((batch, N_STATE))],
        out_specs=[pl.BlockSpec((rows, D_MODEL), lambda k: (k, 0)),
                   pl.BlockSpec((batch, N_STATE), lambda k: (0, 0)),
                   pl.BlockSpec((batch, N_STATE), lambda k: (0, 0))],
        out_shape=[jax.ShapeDtypeStruct((rows_total, D_MODEL), BF16), state, state],
        scratch_shapes=[pltpu.VMEM((rows, D_MODEL), F32), pltpu.VMEM((rows, 2 * ST_CHUNK), F32)],
        compiler_params=_params("arbitrary"),
        name="s5",
    )(u_tm, gain.reshape(1, -1), jnp.broadcast_to(abr.reshape(1, -1), (batch, N_STATE)),
      jnp.broadcast_to(abi.reshape(1, -1), (batch, N_STATE)), wb, wcr, wci,
      d_skip.reshape(1, -1), h0r, h0i)


def _tail_body(h_ref, a_ref, p_ref, *rest, mixer, hid_chunks):
    n_mix = 1 if mixer == "attn" else 2
    mix_w = rest[:n_mix]
    (nf_ref, wg_ref, wu_ref, wd_ref, np_ref, wpg_ref, wpp_ref, o_ref) = rest[n_mix:]
    a = a_ref[...]
    if mixer == "attn":
        mix = jnp.dot(a, mix_w[0][...], preferred_element_type=F32)
    else:
        mix = (jnp.dot(a, mix_w[0][...], preferred_element_type=F32)
               * jax.nn.sigmoid(jnp.dot(a, mix_w[1][...], preferred_element_type=F32)))
    h = h_ref[...] + mix
    x = _rms(h, nf_ref[...]).astype(BF16)
    ffn = None
    for lo, hi in hid_chunks:
        act = (jax.nn.silu(jnp.dot(x, wg_ref[:, lo:hi], preferred_element_type=F32))
               * jnp.dot(x, wu_ref[:, lo:hi], preferred_element_type=F32)).astype(BF16)
        part = jnp.dot(act, wd_ref[lo:hi, :], preferred_element_type=F32)
        ffn = part if ffn is None else ffn + part
    h = h + ffn
    gate = jax.nn.sigmoid(jnp.dot(_rms(h, np_ref[...]).astype(BF16), wpg_ref[...], preferred_element_type=F32))
    o_ref[...] = h + gate * jnp.dot(p_ref[...].astype(BF16), wpp_ref[...], preferred_element_type=F32)


def _tail(h, a, p_all, layer, mix_w, norm_ffn, w_gate, w_up, w_down, norm_ple, w_ple_gate, w_ple_proj,
          *, mixer, batch, tm, in_layout="bm", out_layout="bm"):
    m = h.shape[0]
    n_blk = m // tm
    n_t = n_blk // batch
    hidden = w_gate.shape[1]
    n_chunk = 2 if hidden % (2 * LANES) == 0 else 1
    step = hidden // n_chunk
    hid_chunks = tuple((i * step, (i + 1) * step) for i in range(n_chunk))
    body = functools.partial(_tail_body, mixer=mixer, hid_chunks=hid_chunks)
    ple = p_all.shape[1]
    out = pl.pallas_call(
        body,
        grid=(n_blk,),
        in_specs=([_row_spec(in_layout, tm, D_MODEL, n_t), _row_spec(in_layout, tm, D_MODEL, n_t),
                   pl.BlockSpec((tm, ple), lambda i: (layer * n_blk + i, 0))]
                  + [_resident(w.shape) for w in mix_w]
                  + [_resident((1, D_MODEL)), _resident(w_gate.shape), _resident(w_up.shape),
                     _resident(w_down.shape), _resident((1, D_MODEL)), _resident(w_ple_gate.shape),
                     _resident(w_ple_proj.shape)]),
        out_specs=_row_spec(out_layout, tm, D_MODEL, n_t),
        out_shape=jax.ShapeDtypeStruct(_row_view(h, out_layout, batch).shape, F32),
        compiler_params=_params("parallel"),
        name="tail_" + mixer,
    )(_row_view(h, in_layout, batch), _row_view(a, in_layout, batch), p_all, *mix_w,
      norm_ffn.reshape(1, -1), w_gate, w_up, w_down, norm_ple.reshape(1, -1), w_ple_gate, w_ple_proj)
    return out.reshape(m, D_MODEL)


def _trunk(x, p, is_sample, wts, cache_k, cache_v, state_re, state_im, page_table):
    batch, seq, _ = x.shape
    m = batch * seq
    tm = min(512, seq) if not is_sample else m
    tail_batch = batch if not is_sample else 1
    h = x.reshape(m, D_MODEL)
    p_all = p.reshape(DEPTH * m, -1)
    h_layout = "bm"
    kv_out = None
    s_re, s_im = [], []
    for i in range(DEPTH):
        j = i // 2
        tail_w = (wts["norm_ffn"][i], wts["w_ffn_gate"][i], wts["w_ffn_up"][i], wts["w_ffn_down"][i],
                  wts["norm_ple"][i], wts["w_ple_gate"][i], wts["w_ple_proj"][i])
        next_is_ssm = (i + 1 < DEPTH) and ((i + 1) % 2 == 1) and not is_sample
        out_layout = "tm" if next_is_ssm else "bm"
        if i % 2 == 0:
            lam0 = _lambda_init(i)
            lam_params = jnp.stack([wts["lam_q1"][j], wts["lam_k1"][j], wts["lam_q2"][j], wts["lam_k2"][j]])
            q, kb, vx, k_out, v_out = _qkv(h, wts["norm_mix"][i], wts["w_qkv"][j], wts["q_norm"][j],
                                           wts["k_norm"][j], kv_out, j, tm=min(512, m))
            kv_out = (k_out, v_out)
            if is_sample:
                vb = vx.reshape(m, N_HEADS, VX_COLS)[:, :, :V_DIM].reshape(m, N_HEADS * V_DIM)
                o = _attn_sample(q, kb, vb, cache_k, cache_v, j, page_table, lam_params, wts["subln"][j],
                                 lam0, seq, pages_per_step=8)
            else:
                o = _attn_prompt(q, kb, vx, lam_params, wts["subln"][j], lam0, batch, seq,
                                 tq=min(512, seq), heads=2)
            h = _tail(h, o, p_all, i, (wts["w_o"][j],), *tail_w, mixer="attn", batch=tail_batch, tm=tm,
                      in_layout="bm", out_layout=out_layout)
            h_layout = out_layout
        else:
            abr, abi, bbr, bbi = _s5_disc(wts["ssm_a_re"][j], wts["ssm_a_im"][j], wts["ssm_log_step"][j],
                                          wts["ssm_b_re"][j], wts["ssm_b_im"][j])
            wb = jnp.concatenate([_block_diag_in(bbr), _block_diag_in(bbi)], axis=-1).astype(BF16)
            wcr = _block_diag_out(wts["ssm_c_re"][j]).astype(BF16)
            wci = _block_diag_out(wts["ssm_c_im"][j]).astype(BF16)
            if is_sample:
                u_tm = h.reshape(batch, seq, D_MODEL).transpose(1, 0, 2).reshape(m, D_MODEL)
                h0r = state_re[j].reshape(batch, N_STATE)
                h0i = state_im[j].reshape(batch, N_STATE)
                steps = seq
            else:
                assert h_layout == "tm"
                u_tm = h
                h0r = h0i = jnp.zeros((batch, N_STATE), F32)
                steps = 128
            g, hr, hi = _s5(u_tm, wts["norm_mix"][i], abr, abi, wb, wcr, wci, wts["ssm_d"][j], h0r, h0i,
                            batch, steps)
            if is_sample:
                g = g.reshape(seq, batch, D_MODEL).transpose(1, 0, 2).reshape(m, D_MODEL)
            s_re.append(hr.reshape(batch, N_GROUPS, STATE))
            s_im.append(hi.reshape(batch, N_GROUPS, STATE))
            h = _tail(h, g, p_all, i, (wts["w_glu_a"][j], wts["w_glu_b"][j]), *tail_w, mixer="ssm",
                      batch=tail_batch, tm=tm, in_layout=h_layout, out_layout=out_layout)
            h_layout = out_layout
    kv5 = lambda a: a.reshape(N_ATTN_LAYERS, batch, seq, N_HEADS, V_DIM)
    return (h.reshape(batch, seq, D_MODEL), kv5(kv_out[0]), kv5(kv_out[1]), jnp.stack(s_re), jnp.stack(s_im))


def kernel(x_prompt, x_sample, cache_k, cache_v, state_ssm_re, state_ssm_im, page_table, p_prompt, p_sample, norm_mix, norm_ffn, norm_ple, w_qkv, q_norm, k_norm, lam_q1, lam_k1, lam_q2, lam_k2, subln, w_o, ssm_a_re, ssm_a_im, ssm_log_step, ssm_b_re, ssm_b_im, ssm_c_re, ssm_c_im, ssm_d, w_glu_a, w_glu_b, w_ffn_gate, w_ffn_up, w_ffn_down, w_ple_proj, w_ple_gate):
    bf = lambda w: w.astype(BF16)
    wts = dict(norm_mix=norm_mix, norm_ffn=norm_ffn, norm_ple=norm_ple, w_qkv=bf(w_qkv), q_norm=q_norm,
               k_norm=k_norm, lam_q1=lam_q1, lam_k1=lam_k1, lam_q2=lam_q2, lam_k2=lam_k2, subln=subln,
               w_o=bf(w_o), ssm_a_re=ssm_a_re, ssm_a_im=ssm_a_im, ssm_log_step=ssm_log_step,
               ssm_b_re=ssm_b_re, ssm_b_im=ssm_b_im, ssm_c_re=ssm_c_re, ssm_c_im=ssm_c_im, ssm_d=ssm_d,
               w_glu_a=bf(w_glu_a), w_glu_b=bf(w_glu_b), w_ffn_gate=bf(w_ffn_gate), w_ffn_up=bf(w_ffn_up),
               w_ffn_down=bf(w_ffn_down), w_ple_proj=bf(w_ple_proj), w_ple_gate=bf(w_ple_gate))
    y_p, k_p, v_p, sr_p, si_p = _trunk(x_prompt, p_prompt, False, wts, None, None, None, None, None)
    y_s, k_s, v_s, sr_s, si_s = _trunk(x_sample, p_sample, True, wts, cache_k, cache_v,
                                       state_ssm_re, state_ssm_im, page_table)
    return (y_p, y_s, k_p, v_p, k_s, v_s, sr_p, si_p, sr_s, si_s)
```

```python
import functools
import math

import jax
import jax.numpy as jnp
from jax import lax
from jax.experimental import pallas as pl
from jax.experimental.pallas import tpu as pltpu

F32 = jnp.float32
BF16 = jnp.bfloat16

D_MODEL = 1024
N_HEADS = 8
HEAD_DIM = 64
V_DIM = 2 * HEAD_DIM
QK_COLS = N_HEADS * 2 * HEAD_DIM
GROUP = 16
STATE = 64
N_GROUPS = D_MODEL // GROUP
N_STATE = N_GROUPS * STATE
PAGE_SIZE = 128
DEPTH = 4
N_ATTN_LAYERS = 2
EPS = 1e-6
NEG = -1e30
LOG2E = math.log2(math.e)

LANES = 128
SUBLANES = 8
MXU_DIM = 256
VMEM_LIMIT = 56 * 1024 * 1024

CH_CHUNK = MXU_DIM
ST_CHUNK = CH_CHUNK // GROUP * STATE
NEW_KEYS_PAD = LANES // N_HEADS
VX_COLS = 2 * V_DIM

def _lambda_init(layer):
    return 0.8 - 0.6 * math.exp(-0.3 * layer)


def _alibi_slopes():
    return 2.0 ** (-8.0 * jnp.arange(1, N_HEADS + 1, dtype=F32) / N_HEADS)


def _rms(x, gain):
    return x * lax.rsqrt(jnp.mean(x * x, axis=-1, keepdims=True) + EPS) * gain


def _resident(shape):
    nd = len(shape)
    return pl.BlockSpec(shape, lambda *_: (0,) * nd, pipeline_mode=pl.Buffered(1))


def _resident_layer(stacked, layer):
    _, rows, cols = stacked.shape
    return pl.BlockSpec((None, rows, cols), lambda *_: (layer, 0, 0), pipeline_mode=pl.Buffered(1))


def _params(*sem):
    return pltpu.CompilerParams(dimension_semantics=sem, vmem_limit_bytes=VMEM_LIMIT)


def _qkv_body(h_ref, g_ref, w_ref, qn_ref, kn_ref, e_ref, *rest):
    q_ref, kb_ref, vx_ref, k_ref, v_ref = rest[-5:]
    hn = _rms(h_ref[...], g_ref[...]).astype(BF16)
    qkv = jnp.dot(hn, w_ref[...], preferred_element_type=F32)
    tm = qkv.shape[0]

    def head_norm(z, gain):
        sq = (z * z).astype(BF16)
        parts = [jnp.dot(sq[:, j * MXU_DIM:(j + 1) * MXU_DIM], e_ref[...], preferred_element_type=F32)
                 for j in range(QK_COLS // MXU_DIM)]
        ss = jnp.concatenate(parts, axis=1)
        return z * lax.rsqrt(ss * (1.0 / HEAD_DIM) + EPS) * gain

    q = head_norm(qkv[:, :QK_COLS], qn_ref[...]) * (HEAD_DIM ** -0.5 * LOG2E)
    k = head_norm(qkv[:, QK_COLS:2 * QK_COLS], kn_ref[...])
    v = qkv[:, 2 * QK_COLS:]
    q_ref[...] = q.astype(BF16)
    kb_ref[...] = k.astype(BF16)
    vb = v.astype(BF16)
    ones = jnp.ones((tm, V_DIM), BF16)
    for h in range(N_HEADS):
        head = slice(h * V_DIM, (h + 1) * V_DIM)
        vx_ref[:, h * VX_COLS:h * VX_COLS + V_DIM] = vb[:, head]
        vx_ref[:, h * VX_COLS + V_DIM:(h + 1) * VX_COLS] = ones
        k_ref[pl.ds(h, tm, stride=N_HEADS), :] = k[:, head]
        v_ref[pl.ds(h, tm, stride=N_HEADS), :] = v[:, head]


def _qkv(h, gain, w, qn, kn, kv_prev, layer, tm):
    m = h.shape[0]
    n_blk = m // tm
    ones_bd = (jnp.arange(MXU_DIM)[:, None] // HEAD_DIM == jnp.arange(MXU_DIM)[None, :] // HEAD_DIM).astype(BF16)
    row = lambda c: pl.BlockSpec((tm, c), lambda i: (i, 0))
    kv_spec = pl.BlockSpec((tm * N_HEADS, V_DIM), lambda i: (layer * n_blk + i, 0))
    kv_shape = jax.ShapeDtypeStruct((N_ATTN_LAYERS * m * N_HEADS, V_DIM), F32)
    aliased = [] if kv_prev is None else list(kv_prev)
    n_in = 6
    return pl.pallas_call(
        _qkv_body,
        grid=(n_blk,),
        in_specs=[row(D_MODEL), _resident((1, D_MODEL)), _resident_layer(w, layer),
                  _resident((1, QK_COLS)), _resident((1, QK_COLS)), _resident((MXU_DIM, MXU_DIM))]
                 + [pl.BlockSpec(memory_space=pl.ANY)] * len(aliased),
        out_specs=[row(QK_COLS), row(QK_COLS), row(N_HEADS * VX_COLS), kv_spec, kv_spec],
        out_shape=[jax.ShapeDtypeStruct((m, QK_COLS), BF16),
                   jax.ShapeDtypeStruct((m, QK_COLS), BF16),
                   jax.ShapeDtypeStruct((m, N_HEADS * VX_COLS), BF16),
                   kv_shape, kv_shape],
        input_output_aliases={n_in + a: 3 + a for a in range(len(aliased))},
        compiler_params=_params("parallel"),
        name="qkv",
    )(h, gain.reshape(1, -1), w, jnp.tile(qn, QK_COLS // HEAD_DIM).reshape(1, -1),
      jnp.tile(kn, QK_COLS // HEAD_DIM).reshape(1, -1), ones_bd, *aliased)


def _diff_lambda(lam_ref, lam_init):
    l = lam_ref[...]
    a = jnp.sum(l[0:1] * l[1:2], axis=-1, keepdims=True)
    b = jnp.sum(l[2:3] * l[3:4], axis=-1, keepdims=True)
    return jnp.exp(a) - jnp.exp(b) + lam_init


def _diff_finish(o1, o2, lam, sub_ref, lam_init):
    return _rms(o1 - lam * o2, sub_ref[...]) * (1.0 - lam_init)


def _attn_prompt_body(q_ref, k_ref, vx_ref, slope_ref, lam_ref, sub_ref, o_ref,
                      q2_ref, s_ref, m_ref, acc_ref, *, tq, heads, lam_init):
    n_q = q_ref.shape[0] // tq
    lam = _diff_lambda(lam_ref, lam_init)
    nt = (((1,), (1,)), ((), ()))
    key_off = lax.broadcasted_iota(jnp.int32, (1, tq), 1).astype(F32)
    slopes = [slope_ref[hh][:, 0:1] * LOG2E for hh in range(heads)]
    qk = lambda hh: slice(hh * V_DIM, (hh + 1) * V_DIM)
    vxc = lambda hh: slice(hh * VX_COLS, (hh + 1) * VX_COLS)

    def q_block(qi, carry):
        q0 = pl.multiple_of(qi * tq, tq)
        for hh in range(heads):
            q = q_ref[pl.ds(q0, tq), qk(hh)]
            lane = lax.broadcasted_iota(jnp.int32, q.shape, 1)
            zero = jnp.zeros_like(q)
            q2_ref[hh, 0:tq, :] = jnp.where(lane < HEAD_DIM, q, zero)
            q2_ref[hh, tq:2 * tq, :] = jnp.where(lane >= HEAD_DIM, q, zero)
        m_ref[...] = jnp.full(m_ref.shape, NEG, F32)

        def scores(j, diagonal):
            k0 = pl.multiple_of(j * tq, tq)
            for hh in range(heads):
                s = lax.dot_general(q2_ref[hh], k_ref[pl.ds(k0, tq), qk(hh)], nt, preferred_element_type=F32)
                s = s + slopes[hh] * (key_off + (j * tq).astype(F32))
                if diagonal:
                    future = (lax.broadcasted_iota(jnp.int32, (tq, tq), 1)
                              > lax.broadcasted_iota(jnp.int32, (tq, tq), 0))
                    s = jnp.where(jnp.concatenate([future, future], axis=0), NEG, s)
                s_ref[hh, j] = s
                part = s[:, 0:LANES]
                for c in range(1, tq // LANES):
                    part = jnp.maximum(part, s[:, c * LANES:(c + 1) * LANES])
                m_ref[hh] = jnp.maximum(m_ref[hh], part)

        def full_chunk(j, c):
            scores(j, False)
            return c

        lax.fori_loop(0, qi, full_chunk, 0)
        scores(qi, True)
        for hh in range(heads):
            m_ref[hh] = jnp.broadcast_to(jnp.max(m_ref[hh], axis=-1, keepdims=True), (2 * tq, LANES))

        def weighted(j, k0):
            out = []
            for hh in range(heads):
                m_row = m_ref[hh]
                p = jnp.exp2(s_ref[hh, j] - jnp.concatenate([m_row] * (tq // LANES), axis=1)).astype(BF16)
                out.append(jnp.dot(p, vx_ref[pl.ds(k0, tq), vxc(hh)], preferred_element_type=F32))
            return out

        for hh, w in enumerate(weighted(0, 0)):
            acc_ref[hh] = w

        def more(j, c):
            for hh, w in enumerate(weighted(j, pl.multiple_of(j * tq, tq))):
                acc_ref[hh] += w
            return c

        lax.fori_loop(1, qi + 1, more, 0)
        for hh in range(heads):
            o1 = acc_ref[hh, 0:tq, 0:V_DIM] / acc_ref[hh, 0:tq, V_DIM:VX_COLS]
            o2 = acc_ref[hh, tq:2 * tq, 0:V_DIM] / acc_ref[hh, tq:2 * tq, V_DIM:VX_COLS]
            o_ref[pl.ds(q0, tq), qk(hh)] = _diff_finish(o1, o2, lam, sub_ref, lam_init).astype(BF16)
        return carry

    lax.fori_loop(0, n_q, q_block, 0)


def _attn_prompt(q, kb, vx, lam_params, subln, lam_init, batch, seq, tq, heads):
    slopes = jnp.broadcast_to(_alibi_slopes()[:, None, None], (N_HEADS, 1, LANES))
    body = functools.partial(_attn_prompt_body, tq=tq, heads=heads, lam_init=lam_init)
    per_head = lambda cols: pl.BlockSpec((seq, heads * cols), lambda b, h: (b, h))
    return pl.pallas_call(
        body,
        grid=(batch, N_HEADS // heads),
        in_specs=[per_head(V_DIM), per_head(V_DIM), per_head(VX_COLS),
                  pl.BlockSpec((heads, 1, LANES), lambda b, h: (h, 0, 0)),
                  pl.BlockSpec((4, HEAD_DIM), lambda b, h: (0, 0)),
                  pl.BlockSpec((1, V_DIM), lambda b, h: (0, 0))],
        out_specs=per_head(V_DIM),
        out_shape=jax.ShapeDtypeStruct(q.shape, BF16),
        scratch_shapes=[pltpu.VMEM((heads, 2 * tq, V_DIM), BF16),
                        pltpu.VMEM((heads, seq // tq, 2 * tq, tq), F32),
                        pltpu.VMEM((heads, 2 * tq, LANES), F32),
                        pltpu.VMEM((heads, 2 * tq, VX_COLS), F32)],
        compiler_params=_params("parallel", "parallel"),
        name="attn_prompt",
    )(q, kb, vx, slopes, lam_params, subln.reshape(1, -1))


def _softmax_update(s, pv, m_ref, l_ref, acc_ref, offset=None):
    m_prev = m_ref[...]
    m_cur = jnp.max(s, axis=-1, keepdims=True)
    if offset is not None:
        m_cur = m_cur + offset
    m_new = jnp.maximum(m_prev, m_cur)
    shift = m_new if offset is None else m_new - offset
    p = jnp.exp2(s - shift)
    alpha = jnp.exp2(m_prev - m_new)
    l_ref[...] = alpha * l_ref[...] + jnp.sum(p, axis=-1, keepdims=True)
    acc_ref[...] = alpha * acc_ref[...] + pv(p.astype(BF16))
    m_ref[...] = m_new


def _attn_sample_body(pt_ref, q_ref, *rest, pages_per_step, past, lam_init, n_tok):
    k_refs = rest[:pages_per_step]
    v_refs = rest[pages_per_step:2 * pages_per_step]
    (kn_ref, vn_ref, b0_ref, bn_ref, rowc_ref, lam_ref, sub_ref, o_ref,
     m_ref, l_ref, acc_ref) = rest[2 * pages_per_step:]
    s_idx = pl.program_id(1)
    nt = (((1,), (1,)), ((), ()))
    rows_p = k_refs[0].shape[0]

    @pl.when(s_idx == 0)
    def _():
        m_ref[...] = jnp.full(m_ref.shape, NEG, F32)
        l_ref[...] = jnp.zeros(l_ref.shape, F32)
        acc_ref[...] = jnp.zeros(acc_ref.shape, F32)

    q = q_ref[...]
    first_key = s_idx * (pages_per_step * PAGE_SIZE) - past
    offset = rowc_ref[:, 0:1] * first_key.astype(F32) - rowc_ref[:, 1:2]
    s = jnp.concatenate(
        [lax.dot_general(q, k_refs[i][...].astype(BF16), nt, preferred_element_type=F32)
         for i in range(pages_per_step)], axis=1) + b0_ref[...]

    def pv(p):
        out = None
        for i in range(pages_per_step):
            part = jnp.dot(p[:, i * rows_p:(i + 1) * rows_p], v_refs[i][...].astype(BF16),
                           preferred_element_type=F32)
            out = part if out is None else out + part
        return out

    _softmax_update(s, pv, m_ref, l_ref, acc_ref, offset)

    @pl.when(s_idx == pl.num_programs(1) - 1)
    def _():
        s_new = lax.dot_general(q, kn_ref[...], nt, preferred_element_type=F32) + bn_ref[...]
        _softmax_update(s_new, lambda p: jnp.dot(p, vn_ref[...], preferred_element_type=F32),
                        m_ref, l_ref, acc_ref)
        lam = _diff_lambda(lam_ref, lam_init)
        half = N_HEADS * n_tok
        o1 = acc_ref[0:half, :] / l_ref[0:half, :]
        o2 = acc_ref[half:2 * half, :] / l_ref[half:2 * half, :]
        o_ref[...] = _diff_finish(o1, o2, lam, sub_ref, lam_init).astype(BF16)


def _attn_sample(q, kb, vb, cache_k, cache_v, layer, page_table, lam_params, subln, lam_init, n_tok,
                 pages_per_step):
    db, n_pages = page_table.shape
    past = n_pages * PAGE_SIZE
    n_pool = cache_k.shape[1]
    rows_q = 2 * N_HEADS * n_tok
    rows_p = PAGE_SIZE * N_HEADS
    slopes = _alibi_slopes() * LOG2E
    q5 = q.reshape(db, n_tok, N_HEADS, 2, HEAD_DIM).transpose(0, 3, 2, 1, 4)
    zq = jnp.zeros_like(q5[:, 0])
    qall = jnp.stack([jnp.concatenate([q5[:, 0], zq], -1), jnp.concatenate([zq, q5[:, 1]], -1)], 1)
    qall = qall.reshape(db, rows_q, V_DIM)
    pad_new = lambda x: jnp.pad(x.reshape(db, n_tok, N_HEADS, V_DIM),
                                ((0, 0), (0, NEW_KEYS_PAD - n_tok), (0, 0), (0, 0))).reshape(db, LANES, V_DIM)
    r = jnp.arange(rows_q)
    r_head, r_tok = (r // n_tok) % N_HEADS, r % n_tok
    r_slope = slopes[r_head]
    c = jnp.arange(pages_per_step * rows_p)
    c_key, c_head = c // N_HEADS, c % N_HEADS
    b0 = jnp.where(r_head[:, None] == c_head[None, :], r_slope[:, None] * c_key[None, :].astype(F32), NEG)
    cn = jnp.arange(LANES)
    n_key, n_head = cn // N_HEADS, cn % N_HEADS
    ok = (r_head[:, None] == n_head[None, :]) & (n_key[None, :] <= r_tok[:, None])
    bn = jnp.where(ok, -r_slope[:, None] * (r_tok[:, None] - n_key[None, :]).astype(F32), NEG)
    rowc = jnp.zeros((rows_q, LANES), F32).at[:, 0].set(r_slope).at[:, 1].set(r_slope * r_tok.astype(F32))

    kc = cache_k.reshape(-1, rows_p, V_DIM)
    vc = cache_v.reshape(-1, rows_p, V_DIM)
    page_spec = lambda i: pl.BlockSpec(
        (None, rows_p, V_DIM), lambda b, s, pt: (layer * n_pool + pt[b, s * pages_per_step + i], 0, 0))
    per_b = lambda rows: pl.BlockSpec((None, rows, V_DIM), lambda b, s, pt: (b, 0, 0))
    const = lambda shape: pl.BlockSpec(shape, lambda b, s, pt: (0,) * len(shape))
    body = functools.partial(_attn_sample_body, pages_per_step=pages_per_step, past=past,
                             lam_init=lam_init, n_tok=n_tok)
    out = pl.pallas_call(
        body,
        grid_spec=pltpu.PrefetchScalarGridSpec(
            num_scalar_prefetch=1,
            grid=(db, n_pages // pages_per_step),
            in_specs=([per_b(rows_q)] + [page_spec(i) for i in range(pages_per_step)] * 2
                      + [per_b(LANES), per_b(LANES), const(b0.shape), const((rows_q, LANES)),
                         const((rows_q, LANES)), const((4, HEAD_DIM)), const((1, V_DIM))]),
            out_specs=pl.BlockSpec((None, N_HEADS * n_tok, V_DIM), lambda b, s, pt: (b, 0, 0)),
            scratch_shapes=[pltpu.VMEM((rows_q, 1), F32), pltpu.VMEM((rows_q, 1), F32),
                            pltpu.VMEM((rows_q, V_DIM), F32)]),
        out_shape=jax.ShapeDtypeStruct((db, N_HEADS * n_tok, V_DIM), BF16),
        compiler_params=_params("parallel", "arbitrary"),
        name="attn_sample",
    )(page_table, qall, *([kc] * pages_per_step), *([vc] * pages_per_step),
      pad_new(kb), pad_new(vb), b0, bn, rowc, lam_params, subln.reshape(1, -1))
    return out.reshape(db, N_HEADS, n_tok, V_DIM).transpose(0, 2, 1, 3).reshape(db * n_tok, N_HEADS * V_DIM)


def _s5_disc_body(ar_ref, ai_ref, ls_ref, br_ref, bi_ref, abr_ref, abi_ref, bbr_ref, bbi_ref):
    ar, ai = ar_ref[...], ai_ref[...]
    dt = jnp.exp(ls_ref[...])
    mag = jnp.exp(ar * dt)
    ang = ai * dt
    abr = mag * jnp.cos(ang)
    abi = mag * jnp.sin(ang)
    den = ar * ar + ai * ai
    nr = abr - 1.0
    fr = (nr * ar + abi * ai) / den
    fi = (abi * ar - nr * ai) / den
    abr_ref[...] = abr
    abi_ref[...] = abi
    br, bi = br_ref[...], bi_ref[...]
    bbr_ref[...] = fr * br - fi * bi
    bbi_ref[...] = fr * bi + fi * br


def _s5_disc(a_re, a_im, log_step, b_re, b_im):
    g, p = a_re.shape
    rows = g * GROUP
    rep = lambda a: jnp.repeat(a, GROUP, axis=0)
    bt = lambda b: b.transpose(0, 2, 1).reshape(rows, p)
    full = lambda shape: pl.BlockSpec(shape, lambda: (0,) * len(shape))
    abr, abi, bbr, bbi = pl.pallas_call(
        _s5_disc_body,
        in_specs=[full((rows, p)), full((rows, p)), full((rows, 1)), full((rows, p)), full((rows, p))],
        out_specs=[full((rows, p))] * 4,
        out_shape=[jax.ShapeDtypeStruct((rows, p), F32)] * 4,
        name="s5_disc",
    )(rep(a_re), rep(a_im), rep(log_step.reshape(g, 1)), bt(b_re), bt(b_im))
    pick = lambda a: a.reshape(g, GROUP, p)[:, 0]
    return pick(abr), pick(abi), bbr.reshape(g, GROUP, p), bbi.reshape(g, GROUP, p)


def _block_diag_in(bb):
    gpc = CH_CHUNK // GROUP
    x = bb.reshape(-1, gpc, GROUP, STATE)
    eye = jnp.eye(gpc, dtype=bb.dtype)
    return jnp.einsum("ngcp,gh->ngchp", x, eye).reshape(-1, CH_CHUNK, ST_CHUNK)


def _block_diag_out(cc):
    gpc = CH_CHUNK // GROUP
    x = cc.reshape(-1, gpc, GROUP, STATE)
    eye = jnp.eye(gpc, dtype=cc.dtype)
    return jnp.einsum("ngcp,gh->ngphc", x, eye).reshape(-1, ST_CHUNK, CH_CHUNK)


def _s5_body(u_ref, g_ref, ar_ref, ai_ref, wb_ref, wcr_ref, wci_ref, d_ref, h0r_ref, h0i_ref,
             y_ref, sr_ref, si_ref, hn_ref, x_ref, *, batch, steps, slab, batch_major):
    @pl.when(pl.program_id(0) == 0)
    def _():
        sr_ref[...] = h0r_ref[...]
        si_ref[...] = h0i_ref[...]

    hn = _rms(u_ref[...], g_ref[...])
    if batch_major:
        hn = pltpu.einshape("btd->tbd", hn).reshape(steps * batch, D_MODEL)
    hn_ref[...] = hn
    n_chunks = D_MODEL // CH_CHUNK
    for c in range(n_chunks):
        ch = slice(c * CH_CHUNK, (c + 1) * CH_CHUNK)
        u_c = hn_ref[:, ch]
        x_ref[...] = jnp.dot(u_c.astype(BF16), wb_ref[c], preferred_element_type=F32)
        for s0 in range(0, ST_CHUNK, slab):
            lanes = slice(c * ST_CHUNK + s0, c * ST_CHUNK + s0 + slab)
            re = slice(s0, s0 + slab)
            im = slice(ST_CHUNK + s0, ST_CHUNK + s0 + slab)

            def step(t, carry):
                h_r, h_i = carry
                a_r = ar_ref[:, lanes]
                a_i = ai_ref[:, lanes]
                rows = pl.ds(pl.multiple_of(t * batch, batch), batch)
                n_r = a_r * h_r - a_i * h_i + x_ref[rows, re]
                n_i = a_r * h_i + a_i * h_r + x_ref[rows, im]
                x_ref[rows, re] = n_r
                x_ref[rows, im] = n_i
                return n_r, n_i

            h_r, h_i = lax.fori_loop(0, steps, step, (sr_ref[:, lanes], si_ref[:, lanes]),
                                     unroll=math.gcd(steps, 4))
            sr_ref[:, lanes] = h_r
            si_ref[:, lanes] = h_i
        y = (jnp.dot(x_ref[:, :ST_CHUNK].astype(BF16), wcr_ref[c], preferred_element_type=F32)
             - jnp.dot(x_ref[:, ST_CHUNK:].astype(BF16), wci_ref[c], preferred_element_type=F32))
        y = jax.nn.gelu(y + d_ref[:, ch] * u_c)
        if batch_major:
            y_ref[:, :, ch] = pltpu.einshape("tbd->btd", y.reshape(steps, batch, CH_CHUNK)).astype(BF16)
        else:
            y_ref[:, ch] = y.astype(BF16)


def _s5(u, gain, abr, abi, wb, wcr, wci, d_skip, h0r, h0i, batch, steps, batch_major):
    rows = steps * batch
    slab = min(ST_CHUNK, SUBLANES * LANES * 8 // batch)
    body = functools.partial(_s5_body, batch=batch, steps=steps, slab=slab, batch_major=batch_major)
    state = jax.ShapeDtypeStruct((batch, N_STATE), F32)
    if batch_major:
        n_steps = u.shape[1] // steps
        io_spec = pl.BlockSpec((batch, steps, D_MODEL), lambda k: (0, k, 0))
    else:
        n_steps = u.shape[0] // rows
        io_spec = pl.BlockSpec((rows, D_MODEL), lambda k: (k, 0))
    return pl.pallas_call(
        body,
        grid=(n_steps,),
        in_specs=[io_spec,
                  _resident((1, D_MODEL)), _resident((batch, N_STATE)), _resident((batch, N_STATE)),
                  _resident(wb.shape), _resident(wcr.shape), _resident(wci.shape),
                  _resident((1, D_MODEL)), _resident((batch, N_STATE)), _resident((batch, N_STATE))],
        out_specs=[io_spec,
                   pl.BlockSpec((batch, N_STATE), lambda k: (0, 0)),
                   pl.BlockSpec((batch, N_STATE), lambda k: (0, 0))],
        out_shape=[jax.ShapeDtypeStruct(u.shape, BF16), state, state],
        scratch_shapes=[pltpu.VMEM((rows, D_MODEL), F32), pltpu.VMEM((rows, 2 * ST_CHUNK), F32)],
        compiler_params=_params("arbitrary"),
        name="s5",
    )(u, gain.reshape(1, -1), jnp.broadcast_to(abr.reshape(1, -1), (batch, N_STATE)),
      jnp.broadcast_to(abi.reshape(1, -1), (batch, N_STATE)), wb, wcr, wci,
      d_skip.reshape(1, -1), h0r, h0i)


def _tail_body(h_ref, a_ref, p_ref, *rest, mixer, hid_chunks):
    n_mix = 1 if mixer == "attn" else 2
    mix_w = rest[:n_mix]
    (nf_ref, wg_ref, wu_ref, wd_ref, np_ref, wpg_ref, wpp_ref, o_ref) = rest[n_mix:]
    a = a_ref[...]
    if mixer == "attn":
        mix = jnp.dot(a, mix_w[0][...], preferred_element_type=F32)
    else:
        mix = (jnp.dot(a, mix_w[0][...], preferred_element_type=F32)
               * jax.nn.sigmoid(jnp.dot(a, mix_w[1][...], preferred_element_type=F32)))
    h = h_ref[...] + mix
    x = _rms(h, nf_ref[...]).astype(BF16)
    ffn = None
    for lo, hi in hid_chunks:
        act = (jax.nn.silu(jnp.dot(x, wg_ref[:, lo:hi], preferred_element_type=F32))
               * jnp.dot(x, wu_ref[:, lo:hi], preferred_element_type=F32)).astype(BF16)
        part = jnp.dot(act, wd_ref[lo:hi, :], preferred_element_type=F32)
        ffn = part if ffn is None else ffn + part
    h = h + ffn
    gate = jax.nn.sigmoid(jnp.dot(_rms(h, np_ref[...]).astype(BF16), wpg_ref[...], preferred_element_type=F32))
    o_ref[...] = h + gate * jnp.dot(p_ref[...].astype(BF16), wpp_ref[...], preferred_element_type=F32)


def _hidden_chunks(hidden):
    tiles = hidden // MXU_DIM
    if hidden % MXU_DIM or tiles < 2:
        return ((0, hidden),)
    cut = (tiles + 1) // 2 * MXU_DIM
    return ((0, cut), (cut, hidden))


def _tail(h, a, p_all, layer, mix_layer, mix_w, norm_ffn, w_gate, w_up, w_down, norm_ple, w_ple_gate,
          w_ple_proj, *, mixer, tm):
    m = h.shape[0]
    n_blk = m // tm
    body = functools.partial(_tail_body, mixer=mixer, hid_chunks=_hidden_chunks(w_gate.shape[2]))
    ple = p_all.shape[1]
    row = lambda c: pl.BlockSpec((tm, c), lambda i: (i, 0))
    return pl.pallas_call(
        body,
        grid=(n_blk,),
        in_specs=([row(D_MODEL), row(D_MODEL), pl.BlockSpec((tm, ple), lambda i: (layer * n_blk + i, 0))]
                  + [_resident_layer(w, mix_layer) for w in mix_w]
                  + [_resident((1, D_MODEL)), _resident_layer(w_gate, layer), _resident_layer(w_up, layer),
                     _resident_layer(w_down, layer), _resident((1, D_MODEL)),
                     _resident_layer(w_ple_gate, layer), _resident_layer(w_ple_proj, layer)]),
        out_specs=row(D_MODEL),
        out_shape=jax.ShapeDtypeStruct(h.shape, F32),
        compiler_params=_params("parallel"),
        name="tail_" + mixer,
    )(h, a, p_all, *mix_w, norm_ffn[layer].reshape(1, -1), w_gate, w_up, w_down,
      norm_ple[layer].reshape(1, -1), w_ple_gate, w_ple_proj)


def _trunk(x, p, is_sample, wts, cache_k, cache_v, state_re, state_im, page_table):
    batch, seq, _ = x.shape
    m = batch * seq
    tm = min(512, m)
    h = x.reshape(m, D_MODEL)
    p_all = p.reshape(DEPTH * m, -1)
    kv_out = None
    s_re, s_im = [], []
    tail_w = (wts["norm_ffn"], wts["w_ffn_gate"], wts["w_ffn_up"], wts["w_ffn_down"],
              wts["norm_ple"], wts["w_ple_gate"], wts["w_ple_proj"])
    for i in range(DEPTH):
        j = i // 2
        if i % 2 == 0:
            lam0 = _lambda_init(i)
            lam_params = jnp.stack([wts["lam_q1"][j], wts["lam_k1"][j], wts["lam_q2"][j], wts["lam_k2"][j]])
            q, kb, vx, k_out, v_out = _qkv(h, wts["norm_mix"][i], wts["w_qkv"], wts["q_norm"][j],
                                           wts["k_norm"][j], kv_out, j, tm=tm)
            kv_out = (k_out, v_out)
            if is_sample:
                vb = vx.reshape(m, N_HEADS, VX_COLS)[:, :, :V_DIM].reshape(m, N_HEADS * V_DIM)
                o = _attn_sample(q, kb, vb, cache_k, cache_v, j, page_table, lam_params, wts["subln"][j],
                                 lam0, seq, pages_per_step=8)
            else:
                o = _attn_prompt(q, kb, vx, lam_params, wts["subln"][j], lam0, batch, seq,
                                 tq=min(512, seq), heads=2)
            h = _tail(h, o, p_all, i, j, (wts["w_o"],), *tail_w, mixer="attn", tm=tm)
        else:
            abr, abi, bbr, bbi = _s5_disc(wts["ssm_a_re"][j], wts["ssm_a_im"][j], wts["ssm_log_step"][j],
                                          wts["ssm_b_re"][j], wts["ssm_b_im"][j])
            wb = jnp.concatenate([_block_diag_in(bbr), _block_diag_in(bbi)], axis=-1).astype(BF16)
            wcr = _block_diag_out(wts["ssm_c_re"][j]).astype(BF16)
            wci = _block_diag_out(wts["ssm_c_im"][j]).astype(BF16)
            if is_sample:
                u = h.reshape(batch, seq, D_MODEL).transpose(1, 0, 2).reshape(m, D_MODEL)
                h0r = state_re[j].reshape(batch, N_STATE)
                h0i = state_im[j].reshape(batch, N_STATE)
                steps = seq
            else:
                u = h.reshape(batch, seq, D_MODEL)
                h0r = h0i = jnp.zeros((batch, N_STATE), F32)
                steps = min(128, seq)
            g, hr, hi = _s5(u, wts["norm_mix"][i], abr, abi, wb, wcr, wci, wts["ssm_d"][j], h0r, h0i,
                            batch, steps, batch_major=not is_sample)
            if is_sample:
                g = g.reshape(seq, batch, D_MODEL).transpose(1, 0, 2)
            s_re.append(hr.reshape(batch, N_GROUPS, STATE))
            s_im.append(hi.reshape(batch, N_GROUPS, STATE))
            h = _tail(h, g.reshape(m, D_MODEL), p_all, i, j, (wts["w_glu_a"], wts["w_glu_b"]), *tail_w,
                      mixer="ssm", tm=tm)
    kv5 = lambda a: a.reshape(N_ATTN_LAYERS, batch, seq, N_HEADS, V_DIM)
    return (h.reshape(batch, seq, D_MODEL), kv5(kv_out[0]), kv5(kv_out[1]), jnp.stack(s_re), jnp.stack(s_im))


def kernel(x_prompt, x_sample, cache_k, cache_v, state_ssm_re, state_ssm_im, page_table, p_prompt, p_sample, norm_mix, norm_ffn, norm_ple, w_qkv, q_norm, k_norm, lam_q1, lam_k1, lam_q2, lam_k2, subln, w_o, ssm_a_re, ssm_a_im, ssm_log_step, ssm_b_re, ssm_b_im, ssm_c_re, ssm_c_im, ssm_d, w_glu_a, w_glu_b, w_ffn_gate, w_ffn_up, w_ffn_down, w_ple_proj, w_ple_gate):
    bf = lambda w: w.astype(BF16)
    wts = dict(norm_mix=norm_mix, norm_ffn=norm_ffn, norm_ple=norm_ple, w_qkv=bf(w_qkv), q_norm=q_norm,
               k_norm=k_norm, lam_q1=lam_q1, lam_k1=lam_k1, lam_q2=lam_q2, lam_k2=lam_k2, subln=subln,
               w_o=bf(w_o), ssm_a_re=ssm_a_re, ssm_a_im=ssm_a_im, ssm_log_step=ssm_log_step,
               ssm_b_re=ssm_b_re, ssm_b_im=ssm_b_im, ssm_c_re=ssm_c_re, ssm_c_im=ssm_c_im, ssm_d=ssm_d,
               w_glu_a=bf(w_glu_a), w_glu_b=bf(w_glu_b), w_ffn_gate=bf(w_ffn_gate), w_ffn_up=bf(w_ffn_up),
               w_ffn_down=bf(w_ffn_down), w_ple_proj=bf(w_ple_proj), w_ple_gate=bf(w_ple_gate))
    y_p, k_p, v_p, sr_p, si_p = _trunk(x_prompt, p_prompt, False, wts, None, None, None, None, None)
    y_s, k_s, v_s, sr_s, si_s = _trunk(x_sample, p_sample, True, wts, cache_k, cache_v,
                                       state_ssm_re, state_ssm_im, page_table)
    return (y_p, y_s, k_p, v_p, k_s, v_s, sr_p, si_p, sr_s, si_s)
```

```python
import functools
import math

import jax
import jax.numpy as jnp
from jax import lax
from jax.experimental import pallas as pl
from jax.experimental.pallas import tpu as pltpu

F32 = jnp.float32
BF16 = jnp.bfloat16

D_MODEL = 1024
N_HEADS = 8
HEAD_DIM = 64
V_DIM = 2 * HEAD_DIM
QK_COLS = N_HEADS * 2 * HEAD_DIM
GROUP = 16
STATE = 64
N_GROUPS = D_MODEL // GROUP
N_STATE = N_GROUPS * STATE
PAGE_SIZE = 128
DEPTH = 4
N_ATTN_LAYERS = 2
EPS = 1e-6
NEG = -1e30
LOG2E = math.log2(math.e)

LANES = 128
SUBLANES = 8
MXU_DIM = 256
VMEM_LIMIT = 56 * 1024 * 1024

CH_CHUNK = MXU_DIM
ST_CHUNK = CH_CHUNK // GROUP * STATE
NEW_KEYS_PAD = LANES // N_HEADS
VX_COLS = 2 * V_DIM

def _lambda_init(layer):
    return 0.8 - 0.6 * math.exp(-0.3 * layer)


def _alibi_slopes():
    return 2.0 ** (-8.0 * jnp.arange(1, N_HEADS + 1, dtype=F32) / N_HEADS)


def _rms(x, gain):
    return x * lax.rsqrt(jnp.mean(x * x, axis=-1, keepdims=True) + EPS) * gain


def _resident(shape):
    nd = len(shape)
    return pl.BlockSpec(shape, lambda *_: (0,) * nd, pipeline_mode=pl.Buffered(1))


def _resident_layer(stacked, layer):
    _, rows, cols = stacked.shape
    return pl.BlockSpec((None, rows, cols), lambda *_: (layer, 0, 0), pipeline_mode=pl.Buffered(1))


def _params(*sem):
    return pltpu.CompilerParams(dimension_semantics=sem, vmem_limit_bytes=VMEM_LIMIT)


def _qkv_body(h_ref, g_ref, w_ref, qn_ref, kn_ref, e_ref, *rest):
    q_ref, kb_ref, vx_ref, k_ref, v_ref = rest[-5:]
    hn = _rms(h_ref[...], g_ref[...]).astype(BF16)
    qkv = jnp.dot(hn, w_ref[...], preferred_element_type=F32)
    tm = qkv.shape[0]

    def head_norm(z, gain):
        sq = (z * z).astype(BF16)
        parts = [jnp.dot(sq[:, j * MXU_DIM:(j + 1) * MXU_DIM], e_ref[...], preferred_element_type=F32)
                 for j in range(QK_COLS // MXU_DIM)]
        ss = jnp.concatenate(parts, axis=1)
        return z * lax.rsqrt(ss * (1.0 / HEAD_DIM) + EPS) * gain

    q = head_norm(qkv[:, :QK_COLS], qn_ref[...]) * (HEAD_DIM ** -0.5 * LOG2E)
    k = head_norm(qkv[:, QK_COLS:2 * QK_COLS], kn_ref[...])
    v = qkv[:, 2 * QK_COLS:]
    q_ref[...] = q.astype(BF16)
    kb_ref[...] = k.astype(BF16)
    vb = v.astype(BF16)
    ones = jnp.ones((tm, V_DIM), BF16)
    for h in range(N_HEADS):
        head = slice(h * V_DIM, (h + 1) * V_DIM)
        vx_ref[:, h * VX_COLS:h * VX_COLS + V_DIM] = vb[:, head]
        vx_ref[:, h * VX_COLS + V_DIM:(h + 1) * VX_COLS] = ones
        k_ref[pl.ds(h, tm, stride=N_HEADS), :] = k[:, head]
        v_ref[pl.ds(h, tm, stride=N_HEADS), :] = v[:, head]


def _qkv(h, gain, w, qn, kn, kv_prev, layer, tm):
    m = h.shape[0]
    n_blk = m // tm
    ones_bd = (jnp.arange(MXU_DIM)[:, None] // HEAD_DIM == jnp.arange(MXU_DIM)[None, :] // HEAD_DIM).astype(BF16)
    row = lambda c: pl.BlockSpec((tm, c), lambda i: (i, 0))
    kv_spec = pl.BlockSpec((tm * N_HEADS, V_DIM), lambda i: (layer * n_blk + i, 0))
    kv_shape = jax.ShapeDtypeStruct((N_ATTN_LAYERS * m * N_HEADS, V_DIM), F32)
    aliased = [] if kv_prev is None else list(kv_prev)
    n_in = 6
    return pl.pallas_call(
        _qkv_body,
        grid=(n_blk,),
        in_specs=[row(D_MODEL), _resident((1, D_MODEL)), _resident_layer(w, layer),
                  _resident((1, QK_COLS)), _resident((1, QK_COLS)), _resident((MXU_DIM, MXU_DIM))]
                 + [pl.BlockSpec(memory_space=pl.ANY)] * len(aliased),
        out_specs=[row(QK_COLS), row(QK_COLS), row(N_HEADS * VX_COLS), kv_spec, kv_spec],
        out_shape=[jax.ShapeDtypeStruct((m, QK_COLS), BF16),
                   jax.ShapeDtypeStruct((m, QK_COLS), BF16),
                   jax.ShapeDtypeStruct((m, N_HEADS * VX_COLS), BF16),
                   kv_shape, kv_shape],
        input_output_aliases={n_in + a: 3 + a for a in range(len(aliased))},
        compiler_params=_params("parallel"),
        name="qkv",
    )(h, gain.reshape(1, -1), w, jnp.tile(qn, QK_COLS // HEAD_DIM).reshape(1, -1),
      jnp.tile(kn, QK_COLS // HEAD_DIM).reshape(1, -1), ones_bd, *aliased)


def _diff_lambda(lam_ref, lam_init):
    l = lam_ref[...]
    a = jnp.sum(l[0:1] * l[1:2], axis=-1, keepdims=True)
    b = jnp.sum(l[2:3] * l[3:4], axis=-1, keepdims=True)
    return jnp.exp(a) - jnp.exp(b) + lam_init


def _diff_finish(o1, o2, lam, sub_ref, lam_init):
    return _rms(o1 - lam * o2, sub_ref[...]) * (1.0 - lam_init)


def _attn_prompt_body(q_ref, k_ref, vx_ref, slope_ref, lam_ref, sub_ref, o_ref,
                      q2_ref, s_ref, m_ref, acc_ref, *, tq, heads, lam_init):
    n_q = q_ref.shape[0] // tq
    lam = _diff_lambda(lam_ref, lam_init)
    nt = (((1,), (1,)), ((), ()))
    key_off = lax.broadcasted_iota(jnp.int32, (1, tq), 1).astype(F32)
    slopes = [slope_ref[hh][:, 0:1] * LOG2E for hh in range(heads)]
    qk = lambda hh: slice(hh * V_DIM, (hh + 1) * V_DIM)
    vxc = lambda hh: slice(hh * VX_COLS, (hh + 1) * VX_COLS)

    def q_block(qi, carry):
        q0 = pl.multiple_of(qi * tq, tq)
        for hh in range(heads):
            q = q_ref[pl.ds(q0, tq), qk(hh)]
            lane = lax.broadcasted_iota(jnp.int32, q.shape, 1)
            zero = jnp.zeros_like(q)
            q2_ref[hh, 0:tq, :] = jnp.where(lane < HEAD_DIM, q, zero)
            q2_ref[hh, tq:2 * tq, :] = jnp.where(lane >= HEAD_DIM, q, zero)
        m_ref[...] = jnp.full(m_ref.shape, NEG, F32)

        def scores(j, diagonal):
            k0 = pl.multiple_of(j * tq, tq)
            for hh in range(heads):
                s = lax.dot_general(q2_ref[hh], k_ref[pl.ds(k0, tq), qk(hh)], nt, preferred_element_type=F32)
                s = s + slopes[hh] * (key_off + jnp.asarray(j * tq, F32))
                if diagonal:
                    future = (lax.broadcasted_iota(jnp.int32, (tq, tq), 1)
                              > lax.broadcasted_iota(jnp.int32, (tq, tq), 0))
                    s = jnp.where(jnp.concatenate([future, future], axis=0), NEG, s)
                s_ref[hh, j] = s
                part = s[:, 0:LANES]
                for c in range(1, tq // LANES):
                    part = jnp.maximum(part, s[:, c * LANES:(c + 1) * LANES])
                m_ref[hh] = jnp.maximum(m_ref[hh], part)

        def full_chunk(j, c):
            scores(j, False)
            return c

        lax.fori_loop(0, qi, full_chunk, 0)
        scores(qi, True)
        for hh in range(heads):
            m_ref[hh] = jnp.broadcast_to(jnp.max(m_ref[hh], axis=-1, keepdims=True), (2 * tq, LANES))

        def weighted(j, k0):
            out = []
            for hh in range(heads):
                m_row = m_ref[hh]
                p = jnp.exp2(s_ref[hh, j] - jnp.concatenate([m_row] * (tq // LANES), axis=1)).astype(BF16)
                out.append(jnp.dot(p, vx_ref[pl.ds(k0, tq), vxc(hh)], preferred_element_type=F32))
            return out

        for hh, w in enumerate(weighted(0, 0)):
            acc_ref[hh] = w

        def more(j, c):
            for hh, w in enumerate(weighted(j, pl.multiple_of(j * tq, tq))):
                acc_ref[hh] += w
            return c

        lax.fori_loop(1, qi + 1, more, 0)
        for hh in range(heads):
            o1 = acc_ref[hh, 0:tq, 0:V_DIM] / acc_ref[hh, 0:tq, V_DIM:VX_COLS]
            o2 = acc_ref[hh, tq:2 * tq, 0:V_DIM] / acc_ref[hh, tq:2 * tq, V_DIM:VX_COLS]
            o_ref[pl.ds(q0, tq), qk(hh)] = _diff_finish(o1, o2, lam, sub_ref, lam_init).astype(BF16)
        return carry

    lax.fori_loop(0, n_q, q_block, 0)


def _attn_prompt(q, kb, vx, lam_params, subln, lam_init, batch, seq, tq, heads):
    slopes = jnp.broadcast_to(_alibi_slopes()[:, None, None], (N_HEADS, 1, LANES))
    body = functools.partial(_attn_prompt_body, tq=tq, heads=heads, lam_init=lam_init)
    per_head = lambda cols: pl.BlockSpec((seq, heads * cols), lambda b, h: (b, h))
    return pl.pallas_call(
        body,
        grid=(batch, N_HEADS // heads),
        in_specs=[per_head(V_DIM), per_head(V_DIM), per_head(VX_COLS),
                  pl.BlockSpec((heads, 1, LANES), lambda b, h: (h, 0, 0)),
                  pl.BlockSpec((4, HEAD_DIM), lambda b, h: (0, 0)),
                  pl.BlockSpec((1, V_DIM), lambda b, h: (0, 0))],
        out_specs=per_head(V_DIM),
        out_shape=jax.ShapeDtypeStruct(q.shape, BF16),
        scratch_shapes=[pltpu.VMEM((heads, 2 * tq, V_DIM), BF16),
                        pltpu.VMEM((heads, seq // tq, 2 * tq, tq), F32),
                        pltpu.VMEM((heads, 2 * tq, LANES), F32),
                        pltpu.VMEM((heads, 2 * tq, VX_COLS), F32)],
        compiler_params=_params("parallel", "parallel"),
        name="attn_prompt",
    )(q, kb, vx, slopes, lam_params, subln.reshape(1, -1))


def _softmax_update(s, pv, m_ref, l_ref, acc_ref, offset=None):
    m_prev = m_ref[...]
    m_cur = jnp.max(s, axis=-1, keepdims=True)
    if offset is not None:
        m_cur = m_cur + offset
    m_new = jnp.maximum(m_prev, m_cur)
    shift = m_new if offset is None else m_new - offset
    p = jnp.exp2(s - shift)
    alpha = jnp.exp2(m_prev - m_new)
    l_ref[...] = alpha * l_ref[...] + jnp.sum(p, axis=-1, keepdims=True)
    acc_ref[...] = alpha * acc_ref[...] + pv(p.astype(BF16))
    m_ref[...] = m_new


def _attn_sample_body(pt_ref, q_ref, *rest, pages_per_step, past, lam_init, n_tok):
    k_refs = rest[:pages_per_step]
    v_refs = rest[pages_per_step:2 * pages_per_step]
    (kn_ref, vn_ref, b0_ref, bn_ref, rowc_ref, lam_ref, sub_ref, o_ref,
     m_ref, l_ref, acc_ref) = rest[2 * pages_per_step:]
    s_idx = pl.program_id(1)
    nt = (((1,), (1,)), ((), ()))
    rows_p = k_refs[0].shape[0]

    @pl.when(s_idx == 0)
    def _():
        m_ref[...] = jnp.full(m_ref.shape, NEG, F32)
        l_ref[...] = jnp.zeros(l_ref.shape, F32)
        acc_ref[...] = jnp.zeros(acc_ref.shape, F32)

    q = q_ref[...]
    first_key = s_idx * (pages_per_step * PAGE_SIZE) - past
    offset = rowc_ref[:, 0:1] * jnp.asarray(first_key, F32) - rowc_ref[:, 1:2]
    s = jnp.concatenate(
        [lax.dot_general(q, k_refs[i][...].astype(BF16), nt, preferred_element_type=F32)
         for i in range(pages_per_step)], axis=1) + b0_ref[...]

    def pv(p):
        out = None
        for i in range(pages_per_step):
            part = jnp.dot(p[:, i * rows_p:(i + 1) * rows_p], v_refs[i][...].astype(BF16),
                           preferred_element_type=F32)
            out = part if out is None else out + part
        return out

    _softmax_update(s, pv, m_ref, l_ref, acc_ref, offset)

    @pl.when(s_idx == pl.num_programs(1) - 1)
    def _():
        s_new = lax.dot_general(q, kn_ref[...], nt, preferred_element_type=F32) + bn_ref[...]
        _softmax_update(s_new, lambda p: jnp.dot(p, vn_ref[...], preferred_element_type=F32),
                        m_ref, l_ref, acc_ref)
        lam = _diff_lambda(lam_ref, lam_init)
        half = N_HEADS * n_tok
        o1 = acc_ref[0:half, :] / l_ref[0:half, :]
        o2 = acc_ref[half:2 * half, :] / l_ref[half:2 * half, :]
        o_ref[...] = _diff_finish(o1, o2, lam, sub_ref, lam_init).astype(BF16)


def _attn_sample(q, kb, vb, cache_k, cache_v, layer, page_table, lam_params, subln, lam_init, n_tok,
                 pages_per_step):
    db, n_pages = page_table.shape
    assert n_pages % pages_per_step == 0
    past = n_pages * PAGE_SIZE
    n_pool = cache_k.shape[1]
    rows_q = 2 * N_HEADS * n_tok
    rows_p = PAGE_SIZE * N_HEADS
    slopes = _alibi_slopes() * LOG2E
    q5 = q.reshape(db, n_tok, N_HEADS, 2, HEAD_DIM).transpose(0, 3, 2, 1, 4)
    zq = jnp.zeros_like(q5[:, 0])
    qall = jnp.stack([jnp.concatenate([q5[:, 0], zq], -1), jnp.concatenate([zq, q5[:, 1]], -1)], 1)
    qall = qall.reshape(db, rows_q, V_DIM)
    pad_new = lambda x: jnp.pad(x.reshape(db, n_tok, N_HEADS, V_DIM),
                                ((0, 0), (0, NEW_KEYS_PAD - n_tok), (0, 0), (0, 0))).reshape(db, LANES, V_DIM)
    r = jnp.arange(rows_q)
    r_head, r_tok = (r // n_tok) % N_HEADS, r % n_tok
    r_slope = slopes[r_head]
    c = jnp.arange(pages_per_step * rows_p)
    c_key, c_head = c // N_HEADS, c % N_HEADS
    b0 = jnp.where(r_head[:, None] == c_head[None, :], r_slope[:, None] * c_key[None, :].astype(F32), NEG)
    cn = jnp.arange(LANES)
    n_key, n_head = cn // N_HEADS, cn % N_HEADS
    ok = (r_head[:, None] == n_head[None, :]) & (n_key[None, :] <= r_tok[:, None])
    bn = jnp.where(ok, -r_slope[:, None] * (r_tok[:, None] - n_key[None, :]).astype(F32), NEG)
    rowc = jnp.zeros((rows_q, LANES), F32).at[:, 0].set(r_slope).at[:, 1].set(r_slope * r_tok.astype(F32))

    kc = cache_k.reshape(-1, rows_p, V_DIM)
    vc = cache_v.reshape(-1, rows_p, V_DIM)
    page_spec = lambda i: pl.BlockSpec(
        (None, rows_p, V_DIM), lambda b, s, pt: (layer * n_pool + pt[b, s * pages_per_step + i], 0, 0))
    per_b = lambda rows: pl.BlockSpec((None, rows, V_DIM), lambda b, s, pt: (b, 0, 0))
    const = lambda shape: pl.BlockSpec(shape, lambda b, s, pt: (0,) * len(shape))
    body = functools.partial(_attn_sample_body, pages_per_step=pages_per_step, past=past,
                             lam_init=lam_init, n_tok=n_tok)
    out = pl.pallas_call(
        body,
        grid_spec=pltpu.PrefetchScalarGridSpec(
            num_scalar_prefetch=1,
            grid=(db, n_pages // pages_per_step),
            in_specs=([per_b(rows_q)] + [page_spec(i) for i in range(pages_per_step)] * 2
                      + [per_b(LANES), per_b(LANES), const(b0.shape), const((rows_q, LANES)),
                         const((rows_q, LANES)), const((4, HEAD_DIM)), const((1, V_DIM))]),
            out_specs=pl.BlockSpec((None, N_HEADS * n_tok, V_DIM), lambda b, s, pt: (b, 0, 0)),
            scratch_shapes=[pltpu.VMEM((rows_q, 1), F32), pltpu.VMEM((rows_q, 1), F32),
                            pltpu.VMEM((rows_q, V_DIM), F32)]),
        out_shape=jax.ShapeDtypeStruct((db, N_HEADS * n_tok, V_DIM), BF16),
        compiler_params=_params("parallel", "arbitrary"),
        name="attn_sample",
    )(page_table, qall, *([kc] * pages_per_step), *([vc] * pages_per_step),
      pad_new(kb), pad_new(vb), b0, bn, rowc, lam_params, subln.reshape(1, -1))
    return out.reshape(db, N_HEADS, n_tok, V_DIM).transpose(0, 2, 1, 3).reshape(db * n_tok, N_HEADS * V_DIM)


def _s5_disc_body(ar_ref, ai_ref, ls_ref, br_ref, bi_ref, abr_ref, abi_ref, bbr_ref, bbi_ref):
    ar, ai = ar_ref[...], ai_ref[...]
    dt = jnp.exp(ls_ref[...])
    mag = jnp.exp(ar * dt)
    ang = ai * dt
    abr = mag * jnp.cos(ang)
    abi = mag * jnp.sin(ang)
    den = ar * ar + ai * ai
    nr = abr - 1.0
    fr = (nr * ar + abi * ai) / den
    fi = (abi * ar - nr * ai) / den
    abr_ref[...] = abr
    abi_ref[...] = abi
    br, bi = br_ref[...], bi_ref[...]
    bbr_ref[...] = fr * br - fi * bi
    bbi_ref[...] = fr * bi + fi * br


def _s5_disc(a_re, a_im, log_step, b_re, b_im):
    g, p = a_re.shape
    rows = g * GROUP
    rep = lambda a: jnp.repeat(a, GROUP, axis=0)
    bt = lambda b: b.transpose(0, 2, 1).reshape(rows, p)
    full = lambda shape: pl.BlockSpec(shape, lambda: (0,) * len(shape))
    abr, abi, bbr, bbi = pl.pallas_call(
        _s5_disc_body,
        in_specs=[full((rows, p)), full((rows, p)), full((rows, 1)), full((rows, p)), full((rows, p))],
        out_specs=[full((rows, p))] * 4,
        out_shape=[jax.ShapeDtypeStruct((rows, p), F32)] * 4,
        name="s5_disc",
    )(rep(a_re), rep(a_im), rep(log_step.reshape(g, 1)), bt(b_re), bt(b_im))
    pick = lambda a: a.reshape(g, GROUP, p)[:, 0]
    return pick(abr), pick(abi), bbr.reshape(g, GROUP, p), bbi.reshape(g, GROUP, p)


def _block_diag_in(bb):
    gpc = CH_CHUNK // GROUP
    x = bb.reshape(-1, gpc, GROUP, STATE)
    eye = jnp.eye(gpc, dtype=bb.dtype)
    return jnp.einsum("ngcp,gh->ngchp", x, eye).reshape(-1, CH_CHUNK, ST_CHUNK)


def _block_diag_out(cc):
    gpc = CH_CHUNK // GROUP
    x = cc.reshape(-1, gpc, GROUP, STATE)
    eye = jnp.eye(gpc, dtype=cc.dtype)
    return jnp.einsum("ngcp,gh->ngphc", x, eye).reshape(-1, ST_CHUNK, CH_CHUNK)


def _s5_body(u_ref, g_ref, ar_ref, ai_ref, wb_ref, wcr_ref, wci_ref, d_ref, h0r_ref, h0i_ref,
             y_ref, sr_ref, si_ref, hn_ref, x_ref, *, batch, steps, slab, batch_major):
    @pl.when(pl.program_id(0) == 0)
    def _():
        sr_ref[...] = h0r_ref[...]
        si_ref[...] = h0i_ref[...]

    hn = _rms(u_ref[...], g_ref[...])
    if batch_major:
        hn = jnp.transpose(hn, (1, 0, 2)).reshape(steps * batch, D_MODEL)
    hn_ref[...] = hn
    n_chunks = D_MODEL // CH_CHUNK
    for c in range(n_chunks):
        ch = slice(c * CH_CHUNK, (c + 1) * CH_CHUNK)
        u_c = hn_ref[:, ch]
        x_ref[...] = jnp.dot(u_c.astype(BF16), wb_ref[c], preferred_element_type=F32)
        for s0 in range(0, ST_CHUNK, slab):
            lanes = slice(c * ST_CHUNK + s0, c * ST_CHUNK + s0 + slab)
            re = slice(s0, s0 + slab)
            im = slice(ST_CHUNK + s0, ST_CHUNK + s0 + slab)

            def step(t, carry):
                h_r, h_i = carry
                a_r = ar_ref[:, lanes]
                a_i = ai_ref[:, lanes]
                rows = pl.ds(pl.multiple_of(t * batch, batch), batch)
                n_r = a_r * h_r - a_i * h_i + x_ref[rows, re]
                n_i = a_r * h_i + a_i * h_r + x_ref[rows, im]
                x_ref[rows, re] = n_r
                x_ref[rows, im] = n_i
                return n_r, n_i

            h_r, h_i = lax.fori_loop(0, steps, step, (sr_ref[:, lanes], si_ref[:, lanes]),
                                     unroll=math.gcd(steps, 4))
            sr_ref[:, lanes] = h_r
            si_ref[:, lanes] = h_i
        y = (jnp.dot(x_ref[:, :ST_CHUNK].astype(BF16), wcr_ref[c], preferred_element_type=F32)
             - jnp.dot(x_ref[:, ST_CHUNK:].astype(BF16), wci_ref[c], preferred_element_type=F32))
        y = jax.nn.gelu(y + d_ref[:, ch] * u_c)
        if batch_major:
            y_ref[:, :, ch] = jnp.transpose(y.reshape(steps, batch, CH_CHUNK), (1, 0, 2)).astype(BF16)
        else:
            y_ref[:, ch] = y.astype(BF16)


def _s5(u, gain, abr, abi, wb, wcr, wci, d_skip, h0r, h0i, batch, steps, batch_major):
    rows = steps * batch
    slab = min(ST_CHUNK, SUBLANES * LANES * 8 // batch)
    body = functools.partial(_s5_body, batch=batch, steps=steps, slab=slab, batch_major=batch_major)
    state = jax.ShapeDtypeStruct((batch, N_STATE), F32)
    if batch_major:
        n_steps = u.shape[1] // steps
        io_spec = pl.BlockSpec((batch, steps, D_MODEL), lambda k: (0, k, 0))
    else:
        n_steps = u.shape[0] // rows
        io_spec = pl.BlockSpec((rows, D_MODEL), lambda k: (k, 0))
    return pl.pallas_call(
        body,
        grid=(n_steps,),
        in_specs=[io_spec,
                  _resident((1, D_MODEL)), _resident((batch, N_STATE)), _resident((batch, N_STATE)),
                  _resident(wb.shape), _resident(wcr.shape), _resident(wci.shape),
                  _resident((1, D_MODEL)), _resident((batch, N_STATE)), _resident((batch, N_STATE))],
        out_specs=[io_spec,
                   pl.BlockSpec((batch, N_STATE), lambda k: (0, 0)),
                   pl.BlockSpec((batch, N_STATE), lambda k: (0, 0))],
        out_shape=[jax.ShapeDtypeStruct(u.shape, BF16), state, state],
        scratch_shapes=[pltpu.VMEM((rows, D_MODEL), F32), pltpu.VMEM((rows, 2 * ST_CHUNK), F32)],
        compiler_params=_params("arbitrary"),
        name="s5",
    )(u, gain.reshape(1, -1), jnp.broadcast_to(abr.reshape(1, -1), (batch, N_STATE)),
      jnp.broadcast_to(abi.reshape(1, -1), (batch, N_STATE)), wb, wcr, wci,
      d_skip.reshape(1, -1), h0r, h0i)


def _tail_body(h_ref, a_ref, p_ref, *rest, mixer, hid_chunks):
    n_mix = 1 if mixer == "attn" else 2
    mix_w = rest[:n_mix]
    (nf_ref, wg_ref, wu_ref, wd_ref, np_ref, wpg_ref, wpp_ref, o_ref) = rest[n_mix:]
    a = a_ref[...]
    if mixer == "attn":
        mix = jnp.dot(a, mix_w[0][...], preferred_element_type=F32)
    else:
        mix = (jnp.dot(a, mix_w[0][...], preferred_element_type=F32)
               * jax.nn.sigmoid(jnp.dot(a, mix_w[1][...], preferred_element_type=F32)))
    h = h_ref[...] + mix
    x = _rms(h, nf_ref[...]).astype(BF16)
    ffn = None
    for lo, hi in hid_chunks:
        act = (jax.nn.silu(jnp.dot(x, wg_ref[:, lo:hi], preferred_element_type=F32))
               * jnp.dot(x, wu_ref[:, lo:hi], preferred_element_type=F32)).astype(BF16)
        part = jnp.dot(act, wd_ref[lo:hi, :], preferred_element_type=F32)
        ffn = part if ffn is None else ffn + part
    h = h + ffn
    gate = jax.nn.sigmoid(jnp.dot(_rms(h, np_ref[...]).astype(BF16), wpg_ref[...], preferred_element_type=F32))
    o_ref[...] = h + gate * jnp.dot(p_ref[...].astype(BF16), wpp_ref[...], preferred_element_type=F32)


def _hidden_chunks(hidden):
    tiles = hidden // MXU_DIM
    if hidden % MXU_DIM or tiles < 2:
        return ((0, hidden),)
    cut = (tiles + 1) // 2 * MXU_DIM
    return ((0, cut), (cut, hidden))


def _tail(h, a, p_all, layer, mix_layer, mix_w, norm_ffn, w_gate, w_up, w_down, norm_ple, w_ple_gate,
          w_ple_proj, *, mixer, tm):
    m = h.shape[0]
    n_blk = m // tm
    body = functools.partial(_tail_body, mixer=mixer, hid_chunks=_hidden_chunks(w_gate.shape[2]))
    ple = p_all.shape[1]
    row = lambda c: pl.BlockSpec((tm, c), lambda i: (i, 0))
    return pl.pallas_call(
        body,
        grid=(n_blk,),
        in_specs=([row(D_MODEL), row(D_MODEL), pl.BlockSpec((tm, ple), lambda i: (layer * n_blk + i, 0))]
                  + [_resident_layer(w, mix_layer) for w in mix_w]
                  + [_resident((1, D_MODEL)), _resident_layer(w_gate, layer), _resident_layer(w_up, layer),
                     _resident_layer(w_down, layer), _resident((1, D_MODEL)),
                     _resident_layer(w_ple_gate, layer), _resident_layer(w_ple_proj, layer)]),
        out_specs=row(D_MODEL),
        out_shape=jax.ShapeDtypeStruct(h.shape, F32),
        compiler_params=_params("parallel"),
        name="tail_" + mixer,
    )(h, a, p_all, *mix_w, norm_ffn[layer].reshape(1, -1), w_gate, w_up, w_down,
      norm_ple[layer].reshape(1, -1), w_ple_gate, w_ple_proj)


def _trunk(x, p, is_sample, wts, cache_k, cache_v, state_re, state_im, page_table):
    batch, seq, _ = x.shape
    m = batch * seq
    tm = min(512, m)
    h = x.reshape(m, D_MODEL)
    p_all = p.reshape(DEPTH * m, -1)
    kv_out = None
    s_re, s_im = [], []
    tail_w = (wts["norm_ffn"], wts["w_ffn_gate"], wts["w_ffn_up"], wts["w_ffn_down"],
              wts["norm_ple"], wts["w_ple_gate"], wts["w_ple_proj"])
    for i in range(DEPTH):
        j = i // 2
        if i % 2 == 0:
            lam0 = _lambda_init(i)
            lam_params = jnp.stack([wts["lam_q1"][j], wts["lam_k1"][j], wts["lam_q2"][j], wts["lam_k2"][j]])
            q, kb, vx, k_out, v_out = _qkv(h, wts["norm_mix"][i], wts["w_qkv"], wts["q_norm"][j],
                                           wts["k_norm"][j], kv_out, j, tm=tm)
            kv_out = (k_out, v_out)
            if is_sample:
                vb = vx.reshape(m, N_HEADS, VX_COLS)[:, :, :V_DIM].reshape(m, N_HEADS * V_DIM)
                o = _attn_sample(q, kb, vb, cache_k, cache_v, j, page_table, lam_params, wts["subln"][j],
                                 lam0, seq, pages_per_step=16)
            else:
                o = _attn_prompt(q, kb, vx, lam_params, wts["subln"][j], lam0, batch, seq,
                                 tq=min(512, seq), heads=2)
            h = _tail(h, o, p_all, i, j, (wts["w_o"],), *tail_w, mixer="attn", tm=tm)
        else:
            abr, abi, bbr, bbi = _s5_disc(wts["ssm_a_re"][j], wts["ssm_a_im"][j], wts["ssm_log_step"][j],
                                          wts["ssm_b_re"][j], wts["ssm_b_im"][j])
            wb = jnp.concatenate([_block_diag_in(bbr), _block_diag_in(bbi)], axis=-1).astype(BF16)
            wcr = _block_diag_out(wts["ssm_c_re"][j]).astype(BF16)
            wci = _block_diag_out(wts["ssm_c_im"][j]).astype(BF16)
            if is_sample:
                u = h.reshape(batch, seq, D_MODEL).transpose(1, 0, 2).reshape(m, D_MODEL)
                h0r = state_re[j].reshape(batch, N_STATE)
                h0i = state_im[j].reshape(batch, N_STATE)
                steps = seq
            else:
                u = h.reshape(batch, seq, D_MODEL)
                h0r = h0i = jnp.zeros((batch, N_STATE), F32)
                steps = min(128, seq)
            g, hr, hi = _s5(u, wts["norm_mix"][i], abr, abi, wb, wcr, wci, wts["ssm_d"][j], h0r, h0i,
                            batch, steps, batch_major=not is_sample)
            if is_sample:
                g = g.reshape(seq, batch, D_MODEL).transpose(1, 0, 2)
            s_re.append(hr.reshape(batch, N_GROUPS, STATE))
            s_im.append(hi.reshape(batch, N_GROUPS, STATE))
            h = _tail(h, g.reshape(m, D_MODEL), p_all, i, j, (wts["w_glu_a"], wts["w_glu_b"]), *tail_w,
                      mixer="ssm", tm=tm)
    kv5 = lambda a: a.reshape(N_ATTN_LAYERS, batch, seq, N_HEADS, V_DIM)
    return (h.reshape(batch, seq, D_MODEL), kv5(kv_out[0]), kv5(kv_out[1]), jnp.stack(s_re), jnp.stack(s_im))


def kernel(x_prompt, x_sample, cache_k, cache_v, state_ssm_re, state_ssm_im, page_table, p_prompt, p_sample, norm_mix, norm_ffn, norm_ple, w_qkv, q_norm, k_norm, lam_q1, lam_k1, lam_q2, lam_k2, subln, w_o, ssm_a_re, ssm_a_im, ssm_log_step, ssm_b_re, ssm_b_im, ssm_c_re, ssm_c_im, ssm_d, w_glu_a, w_glu_b, w_ffn_gate, w_ffn_up, w_ffn_down, w_ple_proj, w_ple_gate):
    bf = lambda w: w.astype(BF16)
    wts = dict(norm_mix=norm_mix, norm_ffn=norm_ffn, norm_ple=norm_ple, w_qkv=bf(w_qkv), q_norm=q_norm,
               k_norm=k_norm, lam_q1=lam_q1, lam_k1=lam_k1, lam_q2=lam_q2, lam_k2=lam_k2, subln=subln,
               w_o=bf(w_o), ssm_a_re=ssm_a_re, ssm_a_im=ssm_a_im, ssm_log_step=ssm_log_step,
               ssm_b_re=ssm_b_re, ssm_b_im=ssm_b_im, ssm_c_re=ssm_c_re, ssm_c_im=ssm_c_im, ssm_d=ssm_d,
               w_glu_a=bf(w_glu_a), w_glu_b=bf(w_glu_b), w_ffn_gate=bf(w_ffn_gate), w_ffn_up=bf(w_ffn_up),
               w_ffn_down=bf(w_ffn_down), w_ple_proj=bf(w_ple_proj), w_ple_gate=bf(w_ple_gate))
    y_p, k_p, v_p, sr_p, si_p = _trunk(x_prompt, p_prompt, False, wts, None, None, None, None, None)
    y_s, k_s, v_s, sr_s, si_s = _trunk(x_sample, p_sample, True, wts, cache_k, cache_v,
                                       state_ssm_re, state_ssm_im, page_table)
    return (y_p, y_s, k_p, v_p, k_s, v_s, sr_p, si_p, sr_s, si_s)
```

```python
import functools
import math

import jax
import jax.numpy as jnp
from jax import lax
from jax.experimental import pallas as pl
from jax.experimental.pallas import tpu as pltpu

F32 = jnp.float32
BF16 = jnp.bfloat16

D_MODEL = 1024
N_HEADS = 8
HEAD_DIM = 64
V_DIM = 2 * HEAD_DIM
QK_COLS = N_HEADS * 2 * HEAD_DIM
GROUP = 16
STATE = 64
N_GROUPS = D_MODEL // GROUP
N_STATE = N_GROUPS * STATE
PAGE_SIZE = 128
DEPTH = 4
N_ATTN_LAYERS = 2
EPS = 1e-6
NEG = -1e30
LOG2E = math.log2(math.e)

LANES = 128
SUBLANES = 8
MXU_DIM = 256
VMEM_LIMIT = 56 * 1024 * 1024

CH_CHUNK = MXU_DIM
ST_CHUNK = CH_CHUNK // GROUP * STATE
NEW_KEYS_PAD = LANES // N_HEADS
VX_COLS = 2 * V_DIM


def _lambda_init(layer):
    return 0.8 - 0.6 * math.exp(-0.3 * layer)


def _alibi_slopes():
    return 2.0 ** (-8.0 * jnp.arange(1, N_HEADS + 1, dtype=F32) / N_HEADS)


def _rms(x, gain):
    return x * lax.rsqrt(jnp.mean(x * x, axis=-1, keepdims=True) + EPS) * gain


def _resident(shape):
    nd = len(shape)
    return pl.BlockSpec(shape, lambda *_: (0,) * nd, pipeline_mode=pl.Buffered(1))


def _resident_layer(stacked, layer):
    _, rows, cols = stacked.shape
    return pl.BlockSpec((None, rows, cols), lambda *_: (layer, 0, 0), pipeline_mode=pl.Buffered(1))


def _params(*sem):
    return pltpu.CompilerParams(dimension_semantics=sem, vmem_limit_bytes=VMEM_LIMIT)


def _qkv_body(h_ref, g_ref, w_ref, qn_ref, kn_ref, e_ref, *rest):
    q_ref, kb_ref, vx_ref, k_ref, v_ref = rest[-5:]
    hn = _rms(h_ref[...], g_ref[...]).astype(BF16)
    qkv = jnp.dot(hn, w_ref[...], preferred_element_type=F32)
    tm = qkv.shape[0]

    def head_norm(z, gain):
        sq = (z * z).astype(BF16)
        parts = [jnp.dot(sq[:, j * MXU_DIM:(j + 1) * MXU_DIM], e_ref[...], preferred_element_type=F32)
                 for j in range(QK_COLS // MXU_DIM)]
        ss = jnp.concatenate(parts, axis=1)
        return z * lax.rsqrt(ss * (1.0 / HEAD_DIM) + EPS) * gain

    q = head_norm(qkv[:, :QK_COLS], qn_ref[...]) * (HEAD_DIM ** -0.5 * LOG2E)
    k = head_norm(qkv[:, QK_COLS:2 * QK_COLS], kn_ref[...])
    v = qkv[:, 2 * QK_COLS:]
    q_ref[...] = q.astype(BF16)
    kb_ref[...] = k.astype(BF16)
    vb = v.astype(BF16)
    ones = jnp.ones((tm, V_DIM), BF16)
    for h in range(N_HEADS):
        head = slice(h * V_DIM, (h + 1) * V_DIM)
        vx_ref[:, h * VX_COLS:h * VX_COLS + V_DIM] = vb[:, head]
        vx_ref[:, h * VX_COLS + V_DIM:(h + 1) * VX_COLS] = ones
        k_ref[pl.ds(h, tm, stride=N_HEADS), :] = k[:, head]
        v_ref[pl.ds(h, tm, stride=N_HEADS), :] = v[:, head]


def _qkv(h, gain, w, qn, kn, kv_prev, layer, tm):
    m = h.shape[0]
    n_blk = m // tm
    ones_bd = (jnp.arange(MXU_DIM)[:, None] // HEAD_DIM == jnp.arange(MXU_DIM)[None, :] // HEAD_DIM).astype(BF16)
    row = lambda c: pl.BlockSpec((tm, c), lambda i: (i, 0))
    kv_spec = pl.BlockSpec((tm * N_HEADS, V_DIM), lambda i: (layer * n_blk + i, 0))
    kv_shape = jax.ShapeDtypeStruct((N_ATTN_LAYERS * m * N_HEADS, V_DIM), F32)
    aliased = [] if kv_prev is None else list(kv_prev)
    n_in = 6
    return pl.pallas_call(
        _qkv_body,
        grid=(n_blk,),
        in_specs=[row(D_MODEL), _resident((1, D_MODEL)), _resident_layer(w, layer),
                  _resident((1, QK_COLS)), _resident((1, QK_COLS)), _resident((MXU_DIM, MXU_DIM))]
                 + [pl.BlockSpec(memory_space=pl.ANY)] * len(aliased),
        out_specs=[row(QK_COLS), row(QK_COLS), row(N_HEADS * VX_COLS), kv_spec, kv_spec],
        out_shape=[jax.ShapeDtypeStruct((m, QK_COLS), BF16),
                   jax.ShapeDtypeStruct((m, QK_COLS), BF16),
                   jax.ShapeDtypeStruct((m, N_HEADS * VX_COLS), BF16),
                   kv_shape, kv_shape],
        input_output_aliases={n_in + a: 3 + a for a in range(len(aliased))},
        compiler_params=_params("parallel"),
        name="qkv",
    )(h, gain.reshape(1, -1), w, jnp.tile(qn, QK_COLS // HEAD_DIM).reshape(1, -1),
      jnp.tile(kn, QK_COLS // HEAD_DIM).reshape(1, -1), ones_bd, *aliased)


def _diff_lambda(lam_ref, lam_init):
    l = lam_ref[...]
    a = jnp.sum(l[0:1] * l[1:2], axis=-1, keepdims=True)
    b = jnp.sum(l[2:3] * l[3:4], axis=-1, keepdims=True)
    return jnp.exp(a) - jnp.exp(b) + lam_init


def _diff_finish(o1, o2, lam, sub_ref, lam_init):
    return _rms(o1 - lam * o2, sub_ref[...]) * (1.0 - lam_init)


def _prompt_units(n_q, n_steps):
    units = []
    for qi in range(n_q):
        units += [("A", qi, j) for j in range(qi + 1)] + [("M", qi, 0)]
        units += [("C", qi, j) for j in range(qi + 1)] + [("F", qi, 0)]
    cost = {"A": 4, "C": 4, "M": 1, "F": 1}
    total = sum(cost[u[0]] for u in units)
    parts, acc, k = [[] for _ in range(n_steps)], 0, 0
    for u in units:
        while k < n_steps - 1 and acc >= total * (k + 1) / n_steps:
            k += 1
        parts[k].append(u)
        acc += cost[u[0]]
    return parts


def _softmax_update(s, pv, m_ref, l_ref, acc_ref, offset=None):
    m_prev = m_ref[...]
    m_cur = jnp.max(s, axis=-1, keepdims=True)
    if offset is not None:
        m_cur = m_cur + offset
    m_new = jnp.maximum(m_prev, m_cur)
    shift = m_new if offset is None else m_new - offset
    p = jnp.exp2(s - shift)
    alpha = jnp.exp2(m_prev - m_new)
    l_ref[...] = alpha * l_ref[...] + jnp.sum(p, axis=-1, keepdims=True)
    acc_ref[...] = alpha * acc_ref[...] + pv(p.astype(BF16))
    m_ref[...] = m_new


def _attn_body(pt_ref, q_ref, k_ref, vx_ref, slope_ref, lam_ref, sub_ref, sq_ref, *rest,
               tq, heads, pages, past, lam_init, n_tok, parts):
    k_refs = rest[:pages]
    v_refs = rest[pages:2 * pages]
    (kn_ref, vn_ref, b0_ref, bn_ref, rowc_ref, o_ref, so_ref,
     q2_ref, s_ref, m_ref, acc_ref, sm_ref, sl_ref, sacc_ref) = rest[2 * pages:]
    step = pl.program_id(2)
    nt = (((1,), (1,)), ((), ()))
    lam = _diff_lambda(lam_ref, lam_init)
    key_off = lax.broadcasted_iota(jnp.int32, (1, tq), 1).astype(F32)
    slopes = [slope_ref[hh][:, 0:1] * LOG2E for hh in range(heads)]
    qk = lambda hh: slice(hh * V_DIM, (hh + 1) * V_DIM)
    vxc = lambda hh: slice(hh * VX_COLS, (hh + 1) * VX_COLS)
    rows_p = k_refs[0].shape[0]

    def unit_a(qi, j):
        rows = slice(qi * tq, (qi + 1) * tq)
        keys = slice(j * tq, (j + 1) * tq)
        for hh in range(heads):
            if j == 0:
                q = q_ref[rows, qk(hh)]
                lane = lax.broadcasted_iota(jnp.int32, q.shape, 1)
                zero = jnp.zeros_like(q)
                q2_ref[hh, 0:tq, :] = jnp.where(lane < HEAD_DIM, q, zero)
                q2_ref[hh, tq:2 * tq, :] = jnp.where(lane >= HEAD_DIM, q, zero)
            s = lax.dot_general(q2_ref[hh], k_ref[keys, qk(hh)], nt, preferred_element_type=F32)
            s = s + slopes[hh] * (key_off + float(j * tq))
            if j == qi:
                future = (lax.broadcasted_iota(jnp.int32, (tq, tq), 1)
                          > lax.broadcasted_iota(jnp.int32, (tq, tq), 0))
                s = jnp.where(jnp.concatenate([future, future], axis=0), NEG, s)
            s_ref[hh, j] = s
            part = s[:, 0:LANES]
            for c in range(1, tq // LANES):
                part = jnp.maximum(part, s[:, c * LANES:(c + 1) * LANES])
            m_ref[hh] = part if j == 0 else jnp.maximum(m_ref[hh], part)

    def unit_m(qi):
        for hh in range(heads):
            m_ref[hh] = jnp.broadcast_to(jnp.max(m_ref[hh], axis=-1, keepdims=True), (2 * tq, LANES))

    def unit_c(qi, j):
        keys = slice(j * tq, (j + 1) * tq)
        for hh in range(heads):
            m_row = m_ref[hh]
            p = jnp.exp2(s_ref[hh, j] - jnp.concatenate([m_row] * (tq // LANES), axis=1)).astype(BF16)
            w = jnp.dot(p, vx_ref[keys, vxc(hh)], preferred_element_type=F32)
            if j == 0:
                acc_ref[hh] = w
            else:
                acc_ref[hh] += w

    def unit_f(qi):
        rows = slice(qi * tq, (qi + 1) * tq)
        for hh in range(heads):
            o1 = acc_ref[hh, 0:tq, 0:V_DIM] / acc_ref[hh, 0:tq, V_DIM:VX_COLS]
            o2 = acc_ref[hh, tq:2 * tq, 0:V_DIM] / acc_ref[hh, tq:2 * tq, V_DIM:VX_COLS]
            o_ref[rows, qk(hh)] = _diff_finish(o1, o2, lam, sub_ref, lam_init).astype(BF16)

    def sample_pages(k):
        if k == 0:
            sm_ref[...] = jnp.full(sm_ref.shape, NEG, F32)
            sl_ref[...] = jnp.zeros(sl_ref.shape, F32)
            sacc_ref[...] = jnp.zeros(sacc_ref.shape, F32)
        q = sq_ref[...]
        offset = rowc_ref[:, 0:1] * float(k * pages * PAGE_SIZE - past) - rowc_ref[:, 1:2]
        s = jnp.concatenate(
            [lax.dot_general(q, k_refs[i][...].astype(BF16), nt, preferred_element_type=F32)
             for i in range(pages)], axis=1) + b0_ref[...]

        def pv(p):
            out = None
            for i in range(pages):
                part = jnp.dot(p[:, i * rows_p:(i + 1) * rows_p], v_refs[i][...].astype(BF16),
                               preferred_element_type=F32)
                out = part if out is None else out + part
            return out

        _softmax_update(s, pv, sm_ref, sl_ref, sacc_ref, offset)

    def sample_finish():
        q = sq_ref[...]
        s_new = lax.dot_general(q, kn_ref[...], nt, preferred_element_type=F32) + bn_ref[...]
        _softmax_update(s_new, lambda p: jnp.dot(p, vn_ref[...], preferred_element_type=F32),
                        sm_ref, sl_ref, sacc_ref)
        half = N_HEADS * n_tok
        o1 = sacc_ref[0:half, :] / sl_ref[0:half, :]
        o2 = sacc_ref[half:2 * half, :] / sl_ref[half:2 * half, :]
        so_ref[...] = _diff_finish(o1, o2, lam, sub_ref, lam_init).astype(BF16)

    run = {"A": unit_a, "C": unit_c, "M": lambda qi, j: unit_m(qi), "F": lambda qi, j: unit_f(qi)}
    for k, part in enumerate(parts):
        @pl.when(step == k)
        def _(k=k, part=part):
            sample_pages(k)
            for kind, qi, j in part:
                run[kind](qi, j)
            if k == len(parts) - 1:
                sample_finish()


def _attention(q, kb, vx, sq, skb, svb, cache_k, cache_v, layer, page_table, lam_params, subln, lam_init,
               batch, seq, n_tok, tq, heads, n_steps):
    db, n_pages = page_table.shape
    head_groups = N_HEADS // heads
    assert db == batch * head_groups and n_pages % n_steps == 0 and seq % tq == 0
    pages = n_pages // n_steps
    past = n_pages * PAGE_SIZE
    n_pool = cache_k.shape[1]
    rows_q = 2 * N_HEADS * n_tok
    rows_p = PAGE_SIZE * N_HEADS
    slopes = _alibi_slopes()
    slopes2 = slopes * LOG2E
    q5 = sq.reshape(db, n_tok, N_HEADS, 2, HEAD_DIM).transpose(0, 3, 2, 1, 4)
    zq = jnp.zeros_like(q5[:, 0])
    qall = jnp.stack([jnp.concatenate([q5[:, 0], zq], -1), jnp.concatenate([zq, q5[:, 1]], -1)], 1)
    qall = qall.reshape(db, rows_q, V_DIM)
    pad_new = lambda x: jnp.pad(x.reshape(db, n_tok, N_HEADS, V_DIM),
                                ((0, 0), (0, NEW_KEYS_PAD - n_tok), (0, 0), (0, 0))).reshape(db, LANES, V_DIM)
    r = jnp.arange(rows_q)
    r_head, r_tok = (r // n_tok) % N_HEADS, r % n_tok
    r_slope = slopes2[r_head]
    c = jnp.arange(pages * rows_p)
    c_key, c_head = c // N_HEADS, c % N_HEADS
    b0 = jnp.where(r_head[:, None] == c_head[None, :], r_slope[:, None] * c_key[None, :].astype(F32), NEG)
    cn = jnp.arange(LANES)
    n_key, n_head = cn // N_HEADS, cn % N_HEADS
    ok = (r_head[:, None] == n_head[None, :]) & (n_key[None, :] <= r_tok[:, None])
    bn = jnp.where(ok, -r_slope[:, None] * (r_tok[:, None] - n_key[None, :]).astype(F32), NEG)
    rowc = jnp.zeros((rows_q, LANES), F32).at[:, 0].set(r_slope).at[:, 1].set(r_slope * r_tok.astype(F32))

    kc = cache_k.reshape(-1, rows_p, V_DIM)
    vc = cache_v.reshape(-1, rows_p, V_DIM)
    sample_of = lambda b, h: b * head_groups + h
    page_spec = lambda i: pl.BlockSpec(
        (None, rows_p, V_DIM),
        lambda b, h, s, pt: (layer * n_pool + pt[sample_of(b, h), s * pages + i], 0, 0))
    per_head = lambda cols: pl.BlockSpec((seq, heads * cols), lambda b, h, s, pt: (b, h))
    per_sample = lambda rows: pl.BlockSpec((None, rows, V_DIM), lambda b, h, s, pt: (sample_of(b, h), 0, 0))
    const = lambda shape: pl.BlockSpec(shape, lambda b, h, s, pt: (0,) * len(shape),
                                       pipeline_mode=pl.Buffered(1))
    body = functools.partial(_attn_body, tq=tq, heads=heads, pages=pages, past=past, lam_init=lam_init,
                             n_tok=n_tok, parts=_prompt_units(seq // tq, n_steps))
    o, so = pl.pallas_call(
        body,
        grid_spec=pltpu.PrefetchScalarGridSpec(
            num_scalar_prefetch=1,
            grid=(batch, head_groups, n_steps),
            in_specs=([per_head(V_DIM), per_head(V_DIM), per_head(VX_COLS),
                       pl.BlockSpec((heads, 1, LANES), lambda b, h, s, pt: (h, 0, 0)),
                       const((4, HEAD_DIM)), const((1, V_DIM)), per_sample(rows_q)]
                      + [page_spec(i) for i in range(pages)] * 2
                      + [per_sample(LANES), per_sample(LANES), const(b0.shape), const((rows_q, LANES)),
                         const((rows_q, LANES))]),
            out_specs=[per_head(V_DIM),
                       pl.BlockSpec((None, N_HEADS * n_tok, V_DIM),
                                    lambda b, h, s, pt: (sample_of(b, h), 0, 0))],
            scratch_shapes=[pltpu.VMEM((heads, 2 * tq, V_DIM), BF16),
                            pltpu.VMEM((heads, seq // tq, 2 * tq, tq), F32),
                            pltpu.VMEM((heads, 2 * tq, LANES), F32),
                            pltpu.VMEM((heads, 2 * tq, VX_COLS), F32),
                            pltpu.VMEM((rows_q, 1), F32), pltpu.VMEM((rows_q, 1), F32),
                            pltpu.VMEM((rows_q, V_DIM), F32)]),
        out_shape=[jax.ShapeDtypeStruct(q.shape, BF16),
                   jax.ShapeDtypeStruct((db, N_HEADS * n_tok, V_DIM), BF16)],
        compiler_params=_params("arbitrary", "arbitrary", "arbitrary"),
        name="attention",
    )(page_table, q, kb, vx, jnp.broadcast_to(slopes[:, None, None], (N_HEADS, 1, LANES)), lam_params,
      subln.reshape(1, -1), qall, *([kc] * pages), *([vc] * pages), pad_new(skb), pad_new(svb), b0, bn, rowc)
    so = so.reshape(db, N_HEADS, n_tok, V_DIM).transpose(0, 2, 1, 3).reshape(db * n_tok, N_HEADS * V_DIM)
    return o, so


def _s5_disc_body(ar_ref, ai_ref, ls_ref, br_ref, bi_ref, abr_ref, abi_ref, bbr_ref, bbi_ref):
    ar, ai = ar_ref[...], ai_ref[...]
    dt = jnp.exp(ls_ref[...])
    mag = jnp.exp(ar * dt)
    ang = ai * dt
    abr = mag * jnp.cos(ang)
    abi = mag * jnp.sin(ang)
    den = ar * ar + ai * ai
    nr = abr - 1.0
    fr = (nr * ar + abi * ai) / den
    fi = (abi * ar - nr * ai) / den
    abr_ref[...] = abr
    abi_ref[...] = abi
    br, bi = br_ref[...], bi_ref[...]
    bbr_ref[...] = fr * br - fi * bi
    bbi_ref[...] = fr * bi + fi * br


def _s5_disc(a_re, a_im, log_step, b_re, b_im):
    g, p = a_re.shape
    rows = g * GROUP
    rep = lambda a: jnp.repeat(a, GROUP, axis=0)
    bt = lambda b: b.transpose(0, 2, 1).reshape(rows, p)
    full = lambda shape: pl.BlockSpec(shape, lambda: (0,) * len(shape))
    abr, abi, bbr, bbi = pl.pallas_call(
        _s5_disc_body,
        in_specs=[full((rows, p)), full((rows, p)), full((rows, 1)), full((rows, p)), full((rows, p))],
        out_specs=[full((rows, p))] * 4,
        out_shape=[jax.ShapeDtypeStruct((rows, p), F32)] * 4,
        name="s5_disc",
    )(rep(a_re), rep(a_im), rep(log_step.reshape(g, 1)), bt(b_re), bt(b_im))
    pick = lambda a: a.reshape(g, GROUP, p)[:, 0]
    return pick(abr), pick(abi), bbr.reshape(g, GROUP, p), bbi.reshape(g, GROUP, p)


def _block_diag_in(bb):
    gpc = CH_CHUNK // GROUP
    x = bb.reshape(-1, gpc, GROUP, STATE)
    eye = jnp.eye(gpc, dtype=bb.dtype)
    return jnp.einsum("ngcp,gh->ngchp", x, eye).reshape(-1, CH_CHUNK, ST_CHUNK)


def _block_diag_out(cc):
    gpc = CH_CHUNK // GROUP
    x = cc.reshape(-1, gpc, GROUP, STATE)
    eye = jnp.eye(gpc, dtype=cc.dtype)
    return jnp.einsum("ngcp,gh->ngphc", x, eye).reshape(-1, ST_CHUNK, CH_CHUNK)


def _s5_body(u_ref, g_ref, ar_ref, ai_ref, wb_ref, wcr_ref, wci_ref, d_ref, h0r_ref, h0i_ref,
             y_ref, sr_ref, si_ref, hn_ref, x_ref, *, batch, steps, slab, batch_major):
    @pl.when(pl.program_id(0) == 0)
    def _():
        sr_ref[...] = h0r_ref[...]
        si_ref[...] = h0i_ref[...]

    hn = _rms(u_ref[...], g_ref[...])
    if batch_major:
        hn = jnp.transpose(hn, (1, 0, 2)).reshape(steps * batch, D_MODEL)
    hn_ref[...] = hn
    n_chunks = D_MODEL // CH_CHUNK
    for c in range(n_chunks):
        ch = slice(c * CH_CHUNK, (c + 1) * CH_CHUNK)
        u_c = hn_ref[:, ch]
        x_ref[...] = jnp.dot(u_c.astype(BF16), wb_ref[c], preferred_element_type=F32)
        for s0 in range(0, ST_CHUNK, slab):
            lanes = slice(c * ST_CHUNK + s0, c * ST_CHUNK + s0 + slab)
            re = slice(s0, s0 + slab)
            im = slice(ST_CHUNK + s0, ST_CHUNK + s0 + slab)

            def step(t, carry):
                h_r, h_i = carry
                a_r = ar_ref[:, lanes]
                a_i = ai_ref[:, lanes]
                rows = pl.ds(pl.multiple_of(t * batch, batch), batch)
                n_r = a_r * h_r - a_i * h_i + x_ref[rows, re]
                n_i = a_r * h_i + a_i * h_r + x_ref[rows, im]
                x_ref[rows, re] = n_r
                x_ref[rows, im] = n_i
                return n_r, n_i

            h_r, h_i = lax.fori_loop(0, steps, step, (sr_ref[:, lanes], si_ref[:, lanes]),
                                     unroll=math.gcd(steps, 4))
            sr_ref[:, lanes] = h_r
            si_ref[:, lanes] = h_i
        y = (jnp.dot(x_ref[:, :ST_CHUNK].astype(BF16), wcr_ref[c], preferred_element_type=F32)
             - jnp.dot(x_ref[:, ST_CHUNK:].astype(BF16), wci_ref[c], preferred_element_type=F32))
        y = jax.nn.gelu(y + d_ref[:, ch] * u_c)
        if batch_major:
            y_ref[:, :, ch] = jnp.transpose(y.reshape(steps, batch, CH_CHUNK), (1, 0, 2)).astype(BF16)
        else:
            y_ref[:, ch] = y.astype(BF16)


def _s5(u, gain, abr, abi, wb, wcr, wci, d_skip, h0r, h0i, batch, steps, batch_major):
    rows = steps * batch
    slab = min(ST_CHUNK, SUBLANES * LANES * 8 // batch)
    body = functools.partial(_s5_body, batch=batch, steps=steps, slab=slab, batch_major=batch_major)
    state = jax.ShapeDtypeStruct((batch, N_STATE), F32)
    if batch_major:
        n_steps = u.shape[1] // steps
        io_spec = pl.BlockSpec((batch, steps, D_MODEL), lambda k: (0, k, 0))
    else:
        n_steps = u.shape[0] // rows
        io_spec = pl.BlockSpec((rows, D_MODEL), lambda k: (k, 0))
    return pl.pallas_call(
        body,
        grid=(n_steps,),
        in_specs=[io_spec,
                  _resident((1, D_MODEL)), _resident((batch, N_STATE)), _resident((batch, N_STATE)),
                  _resident(wb.shape), _resident(wcr.shape), _resident(wci.shape),
                  _resident((1, D_MODEL)), _resident((batch, N_STATE)), _resident((batch, N_STATE))],
        out_specs=[io_spec,
                   pl.BlockSpec((batch, N_STATE), lambda k: (0, 0)),
                   pl.BlockSpec((batch, N_STATE), lambda k: (0, 0))],
        out_shape=[jax.ShapeDtypeStruct(u.shape, BF16), state, state],
        scratch_shapes=[pltpu.VMEM((rows, D_MODEL), F32), pltpu.VMEM((rows, 2 * ST_CHUNK), F32)],
        compiler_params=_params("arbitrary"),
        name="s5",
    )(u, gain.reshape(1, -1), jnp.broadcast_to(abr.reshape(1, -1), (batch, N_STATE)),
      jnp.broadcast_to(abi.reshape(1, -1), (batch, N_STATE)), wb, wcr, wci,
      d_skip.reshape(1, -1), h0r, h0i)


def _tail_body(h_ref, a_ref, p_ref, *rest, mixer, hid_chunks):
    n_mix = 1 if mixer == "attn" else 2
    mix_w = rest[:n_mix]
    (nf_ref, wg_ref, wu_ref, wd_ref, np_ref, wpg_ref, wpp_ref, o_ref) = rest[n_mix:]
    a = a_ref[...]
    if mixer == "attn":
        mix = jnp.dot(a, mix_w[0][...], preferred_element_type=F32)
    else:
        mix = (jnp.dot(a, mix_w[0][...], preferred_element_type=F32)
               * jax.nn.sigmoid(jnp.dot(a, mix_w[1][...], preferred_element_type=F32)))
    h = h_ref[...] + mix
    x = _rms(h, nf_ref[...]).astype(BF16)
    ffn = None
    for lo, hi in hid_chunks:
        act = (jax.nn.silu(jnp.dot(x, wg_ref[:, lo:hi], preferred_element_type=F32))
               * jnp.dot(x, wu_ref[:, lo:hi], preferred_element_type=F32)).astype(BF16)
        part = jnp.dot(act, wd_ref[lo:hi, :], preferred_element_type=F32)
        ffn = part if ffn is None else ffn + part
    h = h + ffn
    gate = jax.nn.sigmoid(jnp.dot(_rms(h, np_ref[...]).astype(BF16), wpg_ref[...], preferred_element_type=F32))
    o_ref[...] = h + gate * jnp.dot(p_ref[...].astype(BF16), wpp_ref[...], preferred_element_type=F32)


def _hidden_chunks(hidden):
    tiles = hidden // MXU_DIM
    if hidden % MXU_DIM or tiles < 2:
        return ((0, hidden),)
    cut = (tiles + 1) // 2 * MXU_DIM
    return ((0, cut), (cut, hidden))


def _tail(h, a, p_all, layer, mix_layer, mix_w, norm_ffn, w_gate, w_up, w_down, norm_ple, w_ple_gate,
          w_ple_proj, *, mixer, tm):
    m = h.shape[0]
    n_blk = m // tm
    body = functools.partial(_tail_body, mixer=mixer, hid_chunks=_hidden_chunks(w_gate.shape[2]))
    ple = p_all.shape[1]
    row = lambda c: pl.BlockSpec((tm, c), lambda i: (i, 0))
    return pl.pallas_call(
        body,
        grid=(n_blk,),
        in_specs=([row(D_MODEL), row(D_MODEL), pl.BlockSpec((tm, ple), lambda i: (layer * n_blk + i, 0))]
                  + [_resident_layer(w, mix_layer) for w in mix_w]
                  + [_resident((1, D_MODEL)), _resident_layer(w_gate, layer), _resident_layer(w_up, layer),
                     _resident_layer(w_down, layer), _resident((1, D_MODEL)),
                     _resident_layer(w_ple_gate, layer), _resident_layer(w_ple_proj, layer)]),
        out_specs=row(D_MODEL),
        out_shape=jax.ShapeDtypeStruct(h.shape, F32),
        compiler_params=_params("parallel"),
        name="tail_" + mixer,
    )(h, a, p_all, *mix_w, norm_ffn[layer].reshape(1, -1), w_gate, w_up, w_down,
      norm_ple[layer].reshape(1, -1), w_ple_gate, w_ple_proj)


def _s5_weights(wts, j):
    abr, abi, bbr, bbi = _s5_disc(wts["ssm_a_re"][j], wts["ssm_a_im"][j], wts["ssm_log_step"][j],
                                  wts["ssm_b_re"][j], wts["ssm_b_im"][j])
    wb = jnp.concatenate([_block_diag_in(bbr), _block_diag_in(bbi)], axis=-1).astype(BF16)
    wcr = _block_diag_out(wts["ssm_c_re"][j]).astype(BF16)
    wci = _block_diag_out(wts["ssm_c_im"][j]).astype(BF16)
    return abr, abi, wb, wcr, wci


def kernel(x_prompt, x_sample, cache_k, cache_v, state_ssm_re, state_ssm_im, page_table, p_prompt, p_sample, norm_mix, norm_ffn, norm_ple, w_qkv, q_norm, k_norm, lam_q1, lam_k1, lam_q2, lam_k2, subln, w_o, ssm_a_re, ssm_a_im, ssm_log_step, ssm_b_re, ssm_b_im, ssm_c_re, ssm_c_im, ssm_d, w_glu_a, w_glu_b, w_ffn_gate, w_ffn_up, w_ffn_down, w_ple_proj, w_ple_gate):
    bf = lambda w: w.astype(BF16)
    wts = dict(ssm_a_re=ssm_a_re, ssm_a_im=ssm_a_im, ssm_log_step=ssm_log_step,
               ssm_b_re=ssm_b_re, ssm_b_im=ssm_b_im, ssm_c_re=ssm_c_re, ssm_c_im=ssm_c_im)
    w_qkv, w_o, w_glu_a, w_glu_b = bf(w_qkv), bf(w_o), bf(w_glu_a), bf(w_glu_b)
    tail_w = (norm_ffn, bf(w_ffn_gate), bf(w_ffn_up), bf(w_ffn_down), norm_ple, bf(w_ple_gate), bf(w_ple_proj))
    batch, seq, _ = x_prompt.shape
    db, n_tok, _ = x_sample.shape
    m_p, m_s = batch * seq, db * n_tok
    tm_p, tm_s = min(512, m_p), min(512, m_s)
    h_p = x_prompt.reshape(m_p, D_MODEL)
    h_s = x_sample.reshape(m_s, D_MODEL)
    pp_all = p_prompt.reshape(DEPTH * m_p, -1)
    ps_all = p_sample.reshape(DEPTH * m_s, -1)
    kv_p = kv_s = None
    states = {"p_re": [], "p_im": [], "s_re": [], "s_im": []}
    for i in range(DEPTH):
        j = i // 2
        if i % 2 == 0:
            lam0 = _lambda_init(i)
            lam_params = jnp.stack([lam_q1[j], lam_k1[j], lam_q2[j], lam_k2[j]])
            q_p, kb_p, vx_p, *kv_p = _qkv(h_p, norm_mix[i], w_qkv, q_norm[j], k_norm[j], kv_p, j, tm=tm_p)
            q_s, kb_s, vx_s, *kv_s = _qkv(h_s, norm_mix[i], w_qkv, q_norm[j], k_norm[j], kv_s, j, tm=tm_s)
            vb_s = vx_s.reshape(m_s, N_HEADS, VX_COLS)[:, :, :V_DIM].reshape(m_s, N_HEADS * V_DIM)
            o_p, o_s = _attention(q_p, kb_p, vx_p, q_s, kb_s, vb_s, cache_k, cache_v, j, page_table, lam_params,
                                  subln[j], lam0, batch, seq, n_tok, tq=min(512, seq), heads=2, n_steps=8)
            h_p = _tail(h_p, o_p, pp_all, i, j, (w_o,), *tail_w, mixer="attn", tm=tm_p)
            h_s = _tail(h_s, o_s, ps_all, i, j, (w_o,), *tail_w, mixer="attn", tm=tm_s)
        else:
            abr, abi, wb, wcr, wci = _s5_weights(wts, j)
            zeros = jnp.zeros((batch, N_STATE), F32)
            g_p, hr, hi = _s5(h_p.reshape(batch, seq, D_MODEL), norm_mix[i], abr, abi, wb, wcr, wci, ssm_d[j],
                              zeros, zeros, batch, min(128, seq), batch_major=True)
            states["p_re"].append(hr.reshape(batch, N_GROUPS, STATE))
            states["p_im"].append(hi.reshape(batch, N_GROUPS, STATE))
            u_s = h_s.reshape(db, n_tok, D_MODEL).transpose(1, 0, 2).reshape(m_s, D_MODEL)
            g_s, hr, hi = _s5(u_s, norm_mix[i], abr, abi, wb, wcr, wci, ssm_d[j],
                              state_ssm_re[j].reshape(db, N_STATE), state_ssm_im[j].reshape(db, N_STATE),
                              db, n_tok, batch_major=False)
            g_s = g_s.reshape(n_tok, db, D_MODEL).transpose(1, 0, 2).reshape(m_s, D_MODEL)
            states["s_re"].append(hr.reshape(db, N_GROUPS, STATE))
            states["s_im"].append(hi.reshape(db, N_GROUPS, STATE))
            h_p = _tail(h_p, g_p.reshape(m_p, D_MODEL), pp_all, i, j, (w_glu_a, w_glu_b), *tail_w,
                        mixer="ssm", tm=tm_p)
            h_s = _tail(h_s, g_s, ps_all, i, j, (w_glu_a, w_glu_b), *tail_w, mixer="ssm", tm=tm_s)
    kv5 = lambda a, b, t: a.reshape(N_ATTN_LAYERS, b, t, N_HEADS, V_DIM)
    return (h_p.reshape(batch, seq, D_MODEL), h_s.reshape(db, n_tok, D_MODEL),
            kv5(kv_p[0], batch, seq), kv5(kv_p[1], batch, seq), kv5(kv_s[0], db, n_tok), kv5(kv_s[1], db, n_tok),
            jnp.stack(states["p_re"]), jnp.stack(states["p_im"]),
            jnp.stack(states["s_re"]), jnp.stack(states["s_im"]))
```

```python
import functools
import math

import jax
import jax.numpy as jnp
from jax import lax
from jax.experimental import pallas as pl
from jax.experimental.pallas import tpu as pltpu

F32 = jnp.float32
BF16 = jnp.bfloat16

D_MODEL = 1024
N_HEADS = 8
HEAD_DIM = 64
V_DIM = 2 * HEAD_DIM
QK_COLS = N_HEADS * 2 * HEAD_DIM
GROUP = 16
STATE = 64
N_GROUPS = D_MODEL // GROUP
N_STATE = N_GROUPS * STATE
PAGE_SIZE = 128
DEPTH = 4
N_ATTN_LAYERS = 2
EPS = 1e-6
NEG = -1e30
LOG2E = math.log2(math.e)

LANES = 128
SUBLANES = 8
MXU_DIM = 256
VMEM_LIMIT = 56 * 1024 * 1024

CH_CHUNK = MXU_DIM
ST_CHUNK = CH_CHUNK // GROUP * STATE
NEW_KEYS_PAD = LANES // N_HEADS
VX_COLS = 2 * V_DIM


def _lambda_init(layer):
    return 0.8 - 0.6 * math.exp(-0.3 * layer)


def _alibi_slopes():
    return 2.0 ** (-8.0 * jnp.arange(1, N_HEADS + 1, dtype=F32) / N_HEADS)


def _rms(x, gain):
    return x * lax.rsqrt(jnp.mean(x * x, axis=-1, keepdims=True) + EPS) * gain


def _resident(shape):
    nd = len(shape)
    return pl.BlockSpec(shape, lambda *_: (0,) * nd, pipeline_mode=pl.Buffered(1))


def _resident_layer(stacked, layer):
    _, rows, cols = stacked.shape
    return pl.BlockSpec((None, rows, cols), lambda *_: (layer, 0, 0), pipeline_mode=pl.Buffered(1))


def _params(*sem):
    return pltpu.CompilerParams(dimension_semantics=sem, vmem_limit_bytes=VMEM_LIMIT)


def _qkv_body(h_ref, g_ref, w_ref, qn_ref, kn_ref, e_ref, *rest):
    q_ref, kb_ref, vx_ref, k_ref, v_ref = rest[-5:]
    hn = _rms(h_ref[...], g_ref[...]).astype(BF16)
    qkv = jnp.dot(hn, w_ref[...], preferred_element_type=F32)
    tm = qkv.shape[0]

    def head_norm(z, gain):
        sq = (z * z).astype(BF16)
        parts = [jnp.dot(sq[:, j * MXU_DIM:(j + 1) * MXU_DIM], e_ref[...], preferred_element_type=F32)
                 for j in range(QK_COLS // MXU_DIM)]
        ss = jnp.concatenate(parts, axis=1)
        return z * lax.rsqrt(ss * (1.0 / HEAD_DIM) + EPS) * gain

    q = head_norm(qkv[:, :QK_COLS], qn_ref[...]) * (HEAD_DIM ** -0.5 * LOG2E)
    k = head_norm(qkv[:, QK_COLS:2 * QK_COLS], kn_ref[...])
    v = qkv[:, 2 * QK_COLS:]
    q_ref[...] = q.astype(BF16)
    kb_ref[...] = k.astype(BF16)
    vb = v.astype(BF16)
    ones = jnp.ones((tm, V_DIM), BF16)
    for h in range(N_HEADS):
        head = slice(h * V_DIM, (h + 1) * V_DIM)
        vx_ref[:, h * VX_COLS:h * VX_COLS + V_DIM] = vb[:, head]
        vx_ref[:, h * VX_COLS + V_DIM:(h + 1) * VX_COLS] = ones
        k_ref[pl.ds(h, tm, stride=N_HEADS), :] = k[:, head]
        v_ref[pl.ds(h, tm, stride=N_HEADS), :] = v[:, head]


def _qkv(h, gain, w, qn, kn, kv_prev, layer, tm):
    m = h.shape[0]
    n_blk = m // tm
    ones_bd = (jnp.arange(MXU_DIM)[:, None] // HEAD_DIM == jnp.arange(MXU_DIM)[None, :] // HEAD_DIM).astype(BF16)
    row = lambda c: pl.BlockSpec((tm, c), lambda i: (i, 0))
    kv_spec = pl.BlockSpec((tm * N_HEADS, V_DIM), lambda i: (layer * n_blk + i, 0))
    kv_shape = jax.ShapeDtypeStruct((N_ATTN_LAYERS * m * N_HEADS, V_DIM), F32)
    aliased = [] if kv_prev is None else list(kv_prev)
    n_in = 6
    return pl.pallas_call(
        _qkv_body,
        grid=(n_blk,),
        in_specs=[row(D_MODEL), _resident((1, D_MODEL)), _resident_layer(w, layer),
                  _resident((1, QK_COLS)), _resident((1, QK_COLS)), _resident((MXU_DIM, MXU_DIM))]
                 + [pl.BlockSpec(memory_space=pl.ANY)] * len(aliased),
        out_specs=[row(QK_COLS), row(QK_COLS), row(N_HEADS * VX_COLS), kv_spec, kv_spec],
        out_shape=[jax.ShapeDtypeStruct((m, QK_COLS), BF16),
                   jax.ShapeDtypeStruct((m, QK_COLS), BF16),
                   jax.ShapeDtypeStruct((m, N_HEADS * VX_COLS), BF16),
                   kv_shape, kv_shape],
        input_output_aliases={n_in + a: 3 + a for a in range(len(aliased))},
        compiler_params=_params("parallel"),
        name="qkv",
    )(h, gain.reshape(1, -1), w, jnp.tile(qn, QK_COLS // HEAD_DIM).reshape(1, -1),
      jnp.tile(kn, QK_COLS // HEAD_DIM).reshape(1, -1), ones_bd, *aliased)


def _diff_lambda(lam_ref, lam_init):
    l = lam_ref[...]
    a = jnp.sum(l[0:1] * l[1:2], axis=-1, keepdims=True)
    b = jnp.sum(l[2:3] * l[3:4], axis=-1, keepdims=True)
    return jnp.exp(a) - jnp.exp(b) + lam_init


def _diff_finish(o1, o2, lam, sub_ref, lam_init):
    return _rms(o1 - lam * o2, sub_ref[...]) * (1.0 - lam_init)


def _prompt_units(n_q, n_steps):
    units = []
    for qi in range(n_q):
        units += [("A", qi, j) for j in range(qi + 1)] + [("M", qi, 0)]
        units += [("C", qi, j) for j in range(qi + 1)] + [("F", qi, 0)]
    cost = {"A": 4, "C": 4, "M": 1, "F": 1}
    total = sum(cost[u[0]] for u in units)
    parts, acc, k = [[] for _ in range(n_steps)], 0, 0
    for u in units:
        while k < n_steps - 1 and acc >= total * (k + 1) / n_steps:
            k += 1
        parts[k].append(u)
        acc += cost[u[0]]
    return parts


def _softmax_update(s, pv, m_ref, l_ref, acc_ref, offset=None):
    m_prev = m_ref[...]
    m_cur = jnp.max(s, axis=-1, keepdims=True)
    if offset is not None:
        m_cur = m_cur + offset
    m_new = jnp.maximum(m_prev, m_cur)
    shift = m_new if offset is None else m_new - offset
    p = jnp.exp2(s - shift)
    alpha = jnp.exp2(m_prev - m_new)
    l_ref[...] = alpha * l_ref[...] + jnp.sum(p, axis=-1, keepdims=True)
    acc_ref[...] = alpha * acc_ref[...] + pv(p.astype(BF16))
    m_ref[...] = m_new


def _attn_body(pt_ref, q_ref, k_ref, vx_ref, slope_ref, lam_ref, sub_ref, sq_ref, *rest,
               tq, heads, pages, past, lam_init, n_tok, parts):
    k_refs = rest[:pages]
    v_refs = rest[pages:2 * pages]
    (kn_ref, vn_ref, b0_ref, bn_ref, rowc_ref, o_ref, so_ref,
     q2_ref, s_ref, m_ref, acc_ref, sm_ref, sl_ref, sacc_ref) = rest[2 * pages:]
    step = pl.program_id(2)
    nt = (((1,), (1,)), ((), ()))
    lam = _diff_lambda(lam_ref, lam_init)
    key_off = lax.broadcasted_iota(jnp.int32, (1, tq), 1).astype(F32)
    slopes = [slope_ref[hh][:, 0:1] * LOG2E for hh in range(heads)]
    qk = lambda hh: slice(hh * V_DIM, (hh + 1) * V_DIM)
    vxc = lambda hh: slice(hh * VX_COLS, (hh + 1) * VX_COLS)
    rows_p = k_refs[0].shape[0]

    def unit_a(qi, j):
        rows = slice(qi * tq, (qi + 1) * tq)
        keys = slice(j * tq, (j + 1) * tq)
        for hh in range(heads):
            if j == 0:
                q = q_ref[rows, qk(hh)]
                lane = lax.broadcasted_iota(jnp.int32, q.shape, 1)
                zero = jnp.zeros_like(q)
                q2_ref[hh, 0:tq, :] = jnp.where(lane < HEAD_DIM, q, zero)
                q2_ref[hh, tq:2 * tq, :] = jnp.where(lane >= HEAD_DIM, q, zero)
            s = lax.dot_general(q2_ref[hh], k_ref[keys, qk(hh)], nt, preferred_element_type=F32)
            s = s + slopes[hh] * (key_off + float(j * tq))
            if j == qi:
                future = (lax.broadcasted_iota(jnp.int32, (tq, tq), 1)
                          > lax.broadcasted_iota(jnp.int32, (tq, tq), 0))
                s = jnp.where(jnp.concatenate([future, future], axis=0), NEG, s)
            s_ref[hh, j] = s
            part = s[:, 0:LANES]
            for c in range(1, tq // LANES):
                part = jnp.maximum(part, s[:, c * LANES:(c + 1) * LANES])
            m_ref[hh] = part if j == 0 else jnp.maximum(m_ref[hh], part)

    def unit_m(qi):
        for hh in range(heads):
            m_ref[hh] = jnp.broadcast_to(jnp.max(m_ref[hh], axis=-1, keepdims=True), (2 * tq, LANES))

    def unit_c(qi, j):
        keys = slice(j * tq, (j + 1) * tq)
        for hh in range(heads):
            m_row = m_ref[hh]
            p = jnp.exp2(s_ref[hh, j] - jnp.concatenate([m_row] * (tq // LANES), axis=1)).astype(BF16)
            w = jnp.dot(p, vx_ref[keys, vxc(hh)], preferred_element_type=F32)
            if j == 0:
                acc_ref[hh] = w
            else:
                acc_ref[hh] += w

    def unit_f(qi):
        rows = slice(qi * tq, (qi + 1) * tq)
        for hh in range(heads):
            o1 = acc_ref[hh, 0:tq, 0:V_DIM] / acc_ref[hh, 0:tq, V_DIM:VX_COLS]
            o2 = acc_ref[hh, tq:2 * tq, 0:V_DIM] / acc_ref[hh, tq:2 * tq, V_DIM:VX_COLS]
            o_ref[rows, qk(hh)] = _diff_finish(o1, o2, lam, sub_ref, lam_init).astype(BF16)

    def sample_pages(k):
        if k == 0:
            sm_ref[...] = jnp.full(sm_ref.shape, NEG, F32)
            sl_ref[...] = jnp.zeros(sl_ref.shape, F32)
            sacc_ref[...] = jnp.zeros(sacc_ref.shape, F32)
        q = sq_ref[...]
        offset = rowc_ref[:, 0:1] * float(k * pages * PAGE_SIZE - past) - rowc_ref[:, 1:2]
        s = jnp.concatenate(
            [lax.dot_general(q, k_refs[i][...].astype(BF16), nt, preferred_element_type=F32)
             for i in range(pages)], axis=1) + b0_ref[...]

        def pv(p):
            out = None
            for i in range(pages):
                part = jnp.dot(p[:, i * rows_p:(i + 1) * rows_p], v_refs[i][...].astype(BF16),
                               preferred_element_type=F32)
                out = part if out is None else out + part
            return out

        _softmax_update(s, pv, sm_ref, sl_ref, sacc_ref, offset)

    def sample_finish():
        q = sq_ref[...]
        s_new = lax.dot_general(q, kn_ref[...], nt, preferred_element_type=F32) + bn_ref[...]
        _softmax_update(s_new, lambda p: jnp.dot(p, vn_ref[...], preferred_element_type=F32),
                        sm_ref, sl_ref, sacc_ref)
        half = N_HEADS * n_tok
        o1 = sacc_ref[0:half, :] / sl_ref[0:half, :]
        o2 = sacc_ref[half:2 * half, :] / sl_ref[half:2 * half, :]
        so_ref[...] = _diff_finish(o1, o2, lam, sub_ref, lam_init).astype(BF16)

    run = {"A": unit_a, "C": unit_c, "M": lambda qi, j: unit_m(qi), "F": lambda qi, j: unit_f(qi)}
    for k, part in enumerate(parts):
        @pl.when(step == k)
        def _(k=k, part=part):
            sample_pages(k)
            for kind, qi, j in part:
                run[kind](qi, j)
            if k == len(parts) - 1:
                sample_finish()


def _attention(q, kb, vx, sq, skb, svb, cache_k, cache_v, layer, page_table, lam_params, subln, lam_init,
               batch, seq, n_tok, tq, heads, n_steps):
    db, n_pages = page_table.shape
    head_groups = N_HEADS // heads
    assert db == batch * head_groups and n_pages % n_steps == 0 and seq % tq == 0
    pages = n_pages // n_steps
    past = n_pages * PAGE_SIZE
    n_pool = cache_k.shape[1]
    rows_q = 2 * N_HEADS * n_tok
    rows_p = PAGE_SIZE * N_HEADS
    slopes = _alibi_slopes()
    slopes2 = slopes * LOG2E
    q5 = sq.reshape(db, n_tok, N_HEADS, 2, HEAD_DIM).transpose(0, 3, 2, 1, 4)
    zq = jnp.zeros_like(q5[:, 0])
    qall = jnp.stack([jnp.concatenate([q5[:, 0], zq], -1), jnp.concatenate([zq, q5[:, 1]], -1)], 1)
    qall = qall.reshape(db, rows_q, V_DIM)
    pad_new = lambda x: jnp.pad(x.reshape(db, n_tok, N_HEADS, V_DIM),
                                ((0, 0), (0, NEW_KEYS_PAD - n_tok), (0, 0), (0, 0))).reshape(db, LANES, V_DIM)
    r = jnp.arange(rows_q)
    r_head, r_tok = (r // n_tok) % N_HEADS, r % n_tok
    r_slope = slopes2[r_head]
    c = jnp.arange(pages * rows_p)
    c_key, c_head = c // N_HEADS, c % N_HEADS
    b0 = jnp.where(r_head[:, None] == c_head[None, :], r_slope[:, None] * c_key[None, :].astype(F32), NEG)
    cn = jnp.arange(LANES)
    n_key, n_head = cn // N_HEADS, cn % N_HEADS
    ok = (r_head[:, None] == n_head[None, :]) & (n_key[None, :] <= r_tok[:, None])
    bn = jnp.where(ok, -r_slope[:, None] * (r_tok[:, None] - n_key[None, :]).astype(F32), NEG)
    rowc = jnp.zeros((rows_q, LANES), F32).at[:, 0].set(r_slope).at[:, 1].set(r_slope * r_tok.astype(F32))

    kc = cache_k.reshape(-1, rows_p, V_DIM)
    vc = cache_v.reshape(-1, rows_p, V_DIM)
    sample_of = lambda b, h: b * head_groups + h
    page_spec = lambda i: pl.BlockSpec(
        (None, rows_p, V_DIM),
        lambda b, h, s, pt: (layer * n_pool + pt[sample_of(b, h), s * pages + i], 0, 0))
    per_head = lambda cols: pl.BlockSpec((seq, heads * cols), lambda b, h, s, pt: (b, h))
    per_sample = lambda rows: pl.BlockSpec((None, rows, V_DIM), lambda b, h, s, pt: (sample_of(b, h), 0, 0))
    const = lambda shape: pl.BlockSpec(shape, lambda b, h, s, pt: (0,) * len(shape),
                                       pipeline_mode=pl.Buffered(1))
    body = functools.partial(_attn_body, tq=tq, heads=heads, pages=pages, past=past, lam_init=lam_init,
                             n_tok=n_tok, parts=_prompt_units(seq // tq, n_steps))
    o, so = pl.pallas_call(
        body,
        grid_spec=pltpu.PrefetchScalarGridSpec(
            num_scalar_prefetch=1,
            grid=(batch, head_groups, n_steps),
            in_specs=([per_head(V_DIM), per_head(V_DIM), per_head(VX_COLS),
                       pl.BlockSpec((heads, 1, LANES), lambda b, h, s, pt: (h, 0, 0)),
                       const((4, HEAD_DIM)), const((1, V_DIM)), per_sample(rows_q)]
                      + [page_spec(i) for i in range(pages)] * 2
                      + [per_sample(LANES), per_sample(LANES), const(b0.shape), const((rows_q, LANES)),
                         const((rows_q, LANES))]),
            out_specs=[per_head(V_DIM),
                       pl.BlockSpec((None, N_HEADS * n_tok, V_DIM),
                                    lambda b, h, s, pt: (sample_of(b, h), 0, 0))],
            scratch_shapes=[pltpu.VMEM((heads, 2 * tq, V_DIM), BF16),
                            pltpu.VMEM((heads, seq // tq, 2 * tq, tq), F32),
                            pltpu.VMEM((heads, 2 * tq, LANES), F32),
                            pltpu.VMEM((heads, 2 * tq, VX_COLS), F32),
                            pltpu.VMEM((rows_q, 1), F32), pltpu.VMEM((rows_q, 1), F32),
                            pltpu.VMEM((rows_q, V_DIM), F32)]),
        out_shape=[jax.ShapeDtypeStruct(q.shape, BF16),
                   jax.ShapeDtypeStruct((db, N_HEADS * n_tok, V_DIM), BF16)],
        compiler_params=_params("arbitrary", "arbitrary", "arbitrary"),
        name="attention",
    )(page_table, q, kb, vx, jnp.broadcast_to(slopes[:, None, None], (N_HEADS, 1, LANES)), lam_params,
      subln.reshape(1, -1), qall, *([kc] * pages), *([vc] * pages), pad_new(skb), pad_new(svb), b0, bn, rowc)
    so = so.reshape(db, N_HEADS, n_tok, V_DIM).transpose(0, 2, 1, 3).reshape(db * n_tok, N_HEADS * V_DIM)
    return o, so


def _s5_disc_body(ar_ref, ai_ref, ls_ref, br_ref, bi_ref, abr_ref, abi_ref, bbr_ref, bbi_ref):
    ar, ai = ar_ref[...], ai_ref[...]
    dt = jnp.exp(ls_ref[...])
    mag = jnp.exp(ar * dt)
    ang = ai * dt
    abr = mag * jnp.cos(ang)
    abi = mag * jnp.sin(ang)
    den = ar * ar + ai * ai
    nr = abr - 1.0
    fr = (nr * ar + abi * ai) / den
    fi = (abi * ar - nr * ai) / den
    abr_ref[...] = abr
    abi_ref[...] = abi
    br, bi = br_ref[...], bi_ref[...]
    bbr_ref[...] = fr * br - fi * bi
    bbi_ref[...] = fr * bi + fi * br


def _s5_disc(a_re, a_im, log_step, b_re, b_im):
    g, p = a_re.shape
    rows = g * GROUP
    rep = lambda a: jnp.repeat(a, GROUP, axis=0)
    bt = lambda b: b.transpose(0, 2, 1).reshape(rows, p)
    full = lambda shape: pl.BlockSpec(shape, lambda: (0,) * len(shape))
    abr, abi, bbr, bbi = pl.pallas_call(
        _s5_disc_body,
        in_specs=[full((rows, p)), full((rows, p)), full((rows, 1)), full((rows, p)), full((rows, p))],
        out_specs=[full((rows, p))] * 4,
        out_shape=[jax.ShapeDtypeStruct((rows, p), F32)] * 4,
        name="s5_disc",
    )(rep(a_re), rep(a_im), rep(log_step.reshape(g, 1)), bt(b_re), bt(b_im))
    pick = lambda a: a.reshape(g, GROUP, p)[:, 0]
    return pick(abr), pick(abi), bbr.reshape(g, GROUP, p), bbi.reshape(g, GROUP, p)


def _block_diag_in(bb):
    gpc = CH_CHUNK // GROUP
    x = bb.reshape(-1, gpc, GROUP, STATE)
    eye = jnp.eye(gpc, dtype=bb.dtype)
    return jnp.einsum("ngcp,gh->ngchp", x, eye).reshape(-1, CH_CHUNK, ST_CHUNK)


def _block_diag_out(cc):
    gpc = CH_CHUNK // GROUP
    x = cc.reshape(-1, gpc, GROUP, STATE)
    eye = jnp.eye(gpc, dtype=cc.dtype)
    return jnp.einsum("ngcp,gh->ngphc", x, eye).reshape(-1, ST_CHUNK, CH_CHUNK)


def _s5_body(u_ref, g_ref, ar_ref, ai_ref, wb_ref, wcr_ref, wci_ref, d_ref, h0r_ref, h0i_ref,
             y_ref, sr_ref, si_ref, hn_ref, x_ref, *, batch, steps, slab, batch_major):
    @pl.when(pl.program_id(0) == 0)
    def _():
        sr_ref[...] = h0r_ref[...]
        si_ref[...] = h0i_ref[...]

    hn = _rms(u_ref[...], g_ref[...])
    if batch_major:
        hn = jnp.transpose(hn, (1, 0, 2)).reshape(steps * batch, D_MODEL)
    hn_ref[...] = hn
    n_chunks = D_MODEL // CH_CHUNK
    for c in range(n_chunks):
        ch = slice(c * CH_CHUNK, (c + 1) * CH_CHUNK)
        u_c = hn_ref[:, ch]
        x_ref[...] = jnp.dot(u_c.astype(BF16), wb_ref[c], preferred_element_type=F32)
        for s0 in range(0, ST_CHUNK, slab):
            lanes = slice(c * ST_CHUNK + s0, c * ST_CHUNK + s0 + slab)
            re = slice(s0, s0 + slab)
            im = slice(ST_CHUNK + s0, ST_CHUNK + s0 + slab)

            def step(t, carry):
                h_r, h_i = carry
                a_r = ar_ref[:, lanes]
                a_i = ai_ref[:, lanes]
                rows = pl.ds(pl.multiple_of(t * batch, batch), batch)
                n_r = a_r * h_r - a_i * h_i + x_ref[rows, re]
                n_i = a_r * h_i + a_i * h_r + x_ref[rows, im]
                x_ref[rows, re] = n_r
                x_ref[rows, im] = n_i
                return n_r, n_i

            h_r, h_i = lax.fori_loop(0, steps, step, (sr_ref[:, lanes], si_ref[:, lanes]),
                                     unroll=math.gcd(steps, 4))
            sr_ref[:, lanes] = h_r
            si_ref[:, lanes] = h_i
        y = (jnp.dot(x_ref[:, :ST_CHUNK].astype(BF16), wcr_ref[c], preferred_element_type=F32)
             - jnp.dot(x_ref[:, ST_CHUNK:].astype(BF16), wci_ref[c], preferred_element_type=F32))
        y = jax.nn.gelu(y + d_ref[:, ch] * u_c)
        if batch_major:
            y_ref[:, :, ch] = jnp.transpose(y.reshape(steps, batch, CH_CHUNK), (1, 0, 2)).astype(BF16)
        else:
            y_ref[:, ch] = y.astype(BF16)


def _s5(u, gain, abr, abi, wb, wcr, wci, d_skip, h0r, h0i, batch, steps, batch_major):
    rows = steps * batch
    slab = min(ST_CHUNK, SUBLANES * LANES * 8 // batch)
    body = functools.partial(_s5_body, batch=batch, steps=steps, slab=slab, batch_major=batch_major)
    state = jax.ShapeDtypeStruct((batch, N_STATE), F32)
    if batch_major:
        n_steps = u.shape[1] // steps
        io_spec = pl.BlockSpec((batch, steps, D_MODEL), lambda k: (0, k, 0))
    else:
        n_steps = u.shape[0] // rows
        io_spec = pl.BlockSpec((rows, D_MODEL), lambda k: (k, 0))
    return pl.pallas_call(
        body,
        grid=(n_steps,),
        in_specs=[io_spec,
                  _resident((1, D_MODEL)), _resident((batch, N_STATE)), _resident((batch, N_STATE)),
                  _resident(wb.shape), _resident(wcr.shape), _resident(wci.shape),
                  _resident((1, D_MODEL)), _resident((batch, N_STATE)), _resident((batch, N_STATE))],
        out_specs=[io_spec,
                   pl.BlockSpec((batch, N_STATE), lambda k: (0, 0)),
                   pl.BlockSpec((batch, N_STATE), lambda k: (0, 0))],
        out_shape=[jax.ShapeDtypeStruct(u.shape, BF16), state, state],
        scratch_shapes=[pltpu.VMEM((rows, D_MODEL), F32), pltpu.VMEM((rows, 2 * ST_CHUNK), F32)],
        compiler_params=_params("arbitrary"),
        name="s5",
    )(u, gain.reshape(1, -1), jnp.broadcast_to(abr.reshape(1, -1), (batch, N_STATE)),
      jnp.broadcast_to(abi.reshape(1, -1), (batch, N_STATE)), wb, wcr, wci,
      d_skip.reshape(1, -1), h0r, h0i)


def _tail_body(h_ref, a_ref, p_ref, *rest, mixer, hid_chunks):
    n_mix = 1 if mixer == "attn" else 2
    mix_w = rest[:n_mix]
    (nf_ref, wg_ref, wu_ref, wd_ref, np_ref, wpg_ref, wpp_ref, o_ref) = rest[n_mix:]
    a = a_ref[...]
    if mixer == "attn":
        mix = jnp.dot(a, mix_w[0][...], preferred_element_type=F32)
    else:
        mix = (jnp.dot(a, mix_w[0][...], preferred_element_type=F32)
               * jax.nn.sigmoid(jnp.dot(a, mix_w[1][...], preferred_element_type=F32)))
    h = h_ref[...] + mix
    x = _rms(h, nf_ref[...]).astype(BF16)
    ffn = None
    for lo, hi in hid_chunks:
        act = (jax.nn.silu(jnp.dot(x, wg_ref[:, lo:hi], preferred_element_type=F32))
               * jnp.dot(x, wu_ref[:, lo:hi], preferred_element_type=F32)).astype(BF16)
        part = jnp.dot(act, wd_ref[lo:hi, :], preferred_element_type=F32)
        ffn = part if ffn is None else ffn + part
    h = h + ffn
    gate = jax.nn.sigmoid(jnp.dot(_rms(h, np_ref[...]).astype(BF16), wpg_ref[...], preferred_element_type=F32))
    o_ref[...] = h + gate * jnp.dot(p_ref[...].astype(BF16), wpp_ref[...], preferred_element_type=F32)


def _hidden_chunks(hidden):
    tiles = hidden // MXU_DIM
    if hidden % MXU_DIM or tiles < 2:
        return ((0, hidden),)
    cut = (tiles + 1) // 2 * MXU_DIM
    return ((0, cut), (cut, hidden))


def _tail(h, a, p_all, layer, mix_layer, mix_w, norm_ffn, w_gate, w_up, w_down, norm_ple, w_ple_gate,
          w_ple_proj, *, mixer, tm):
    m = h.shape[0]
    n_blk = m // tm
    body = functools.partial(_tail_body, mixer=mixer, hid_chunks=_hidden_chunks(w_gate.shape[2]))
    ple = p_all.shape[1]
    row = lambda c: pl.BlockSpec((tm, c), lambda i: (i, 0))
    return pl.pallas_call(
        body,
        grid=(n_blk,),
        in_specs=([row(D_MODEL), row(D_MODEL), pl.BlockSpec((tm, ple), lambda i: (layer * n_blk + i, 0))]
                  + [_resident_layer(w, mix_layer) for w in mix_w]
                  + [_resident((1, D_MODEL)), _resident_layer(w_gate, layer), _resident_layer(w_up, layer),
                     _resident_layer(w_down, layer), _resident((1, D_MODEL)),
                     _resident_layer(w_ple_gate, layer), _resident_layer(w_ple_proj, layer)]),
        out_specs=row(D_MODEL),
        out_shape=jax.ShapeDtypeStruct(h.shape, F32),
        compiler_params=_params("parallel"),
        name="tail_" + mixer,
    )(h, a, p_all, *mix_w, norm_ffn[layer].reshape(1, -1), w_gate, w_up, w_down,
      norm_ple[layer].reshape(1, -1), w_ple_gate, w_ple_proj)


def _s5_weights(wts, j):
    abr, abi, bbr, bbi = _s5_disc(wts["ssm_a_re"][j], wts["ssm_a_im"][j], wts["ssm_log_step"][j],
                                  wts["ssm_b_re"][j], wts["ssm_b_im"][j])
    wb = jnp.concatenate([_block_diag_in(bbr), _block_diag_in(bbi)], axis=-1).astype(BF16)
    wcr = _block_diag_out(wts["ssm_c_re"][j]).astype(BF16)
    wci = _block_diag_out(wts["ssm_c_im"][j]).astype(BF16)
    return abr, abi, wb, wcr, wci


def kernel(x_prompt, x_sample, cache_k, cache_v, state_ssm_re, state_ssm_im, page_table, p_prompt, p_sample, norm_mix, norm_ffn, norm_ple, w_qkv, q_norm, k_norm, lam_q1, lam_k1, lam_q2, lam_k2, subln, w_o, ssm_a_re, ssm_a_im, ssm_log_step, ssm_b_re, ssm_b_im, ssm_c_re, ssm_c_im, ssm_d, w_glu_a, w_glu_b, w_ffn_gate, w_ffn_up, w_ffn_down, w_ple_proj, w_ple_gate):
    bf = lambda w: w.astype(BF16)
    wts = dict(ssm_a_re=ssm_a_re, ssm_a_im=ssm_a_im, ssm_log_step=ssm_log_step,
               ssm_b_re=ssm_b_re, ssm_b_im=ssm_b_im, ssm_c_re=ssm_c_re, ssm_c_im=ssm_c_im)
    w_qkv, w_o, w_glu_a, w_glu_b = bf(w_qkv), bf(w_o), bf(w_glu_a), bf(w_glu_b)
    tail_w = (norm_ffn, bf(w_ffn_gate), bf(w_ffn_up), bf(w_ffn_down), norm_ple, bf(w_ple_gate), bf(w_ple_proj))
    batch, seq, _ = x_prompt.shape
    db, n_tok, _ = x_sample.shape
    m_p, m_s = batch * seq, db * n_tok
    tm_p, tm_s = min(512, m_p), min(512, m_s)
    h_p = x_prompt.reshape(m_p, D_MODEL)
    h_s = x_sample.reshape(m_s, D_MODEL)
    pp_all = p_prompt.reshape(DEPTH * m_p, -1)
    ps_all = p_sample.reshape(DEPTH * m_s, -1)
    kv_p = kv_s = None
    states = {"p_re": [], "p_im": [], "s_re": [], "s_im": []}
    for i in range(DEPTH):
        j = i // 2
        if i % 2 == 0:
            lam0 = _lambda_init(i)
            lam_params = jnp.stack([lam_q1[j], lam_k1[j], lam_q2[j], lam_k2[j]])
            q_p, kb_p, vx_p, *kv_p = _qkv(h_p, norm_mix[i], w_qkv, q_norm[j], k_norm[j], kv_p, j, tm=tm_p)
            q_s, kb_s, vx_s, *kv_s = _qkv(h_s, norm_mix[i], w_qkv, q_norm[j], k_norm[j], kv_s, j, tm=tm_s)
            vb_s = vx_s.reshape(m_s, N_HEADS, VX_COLS)[:, :, :V_DIM].reshape(m_s, N_HEADS * V_DIM)
            o_p, o_s = _attention(q_p, kb_p, vx_p, q_s, kb_s, vb_s, cache_k, cache_v, j, page_table, lam_params,
                                  subln[j], lam0, batch, seq, n_tok, tq=min(256, seq), heads=2, n_steps=8)
            h_p = _tail(h_p, o_p, pp_all, i, j, (w_o,), *tail_w, mixer="attn", tm=tm_p)
            h_s = _tail(h_s, o_s, ps_all, i, j, (w_o,), *tail_w, mixer="attn", tm=tm_s)
        else:
            abr, abi, wb, wcr, wci = _s5_weights(wts, j)
            zeros = jnp.zeros((batch, N_STATE), F32)
            g_p, hr, hi = _s5(h_p.reshape(batch, seq, D_MODEL), norm_mix[i], abr, abi, wb, wcr, wci, ssm_d[j],
                              zeros, zeros, batch, min(128, seq), batch_major=True)
            states["p_re"].append(hr.reshape(batch, N_GROUPS, STATE))
            states["p_im"].append(hi.reshape(batch, N_GROUPS, STATE))
            u_s = h_s.reshape(db, n_tok, D_MODEL).transpose(1, 0, 2).reshape(m_s, D_MODEL)
            g_s, hr, hi = _s5(u_s, norm_mix[i], abr, abi, wb, wcr, wci, ssm_d[j],
                              state_ssm_re[j].reshape(db, N_STATE), state_ssm_im[j].reshape(db, N_STATE),
                              db, n_tok, batch_major=False)
            g_s = g_s.reshape(n_tok, db, D_MODEL).transpose(1, 0, 2).reshape(m_s, D_MODEL)
            states["s_re"].append(hr.reshape(db, N_GROUPS, STATE))
            states["s_im"].append(hi.reshape(db, N_GROUPS, STATE))
            h_p = _tail(h_p, g_p.reshape(m_p, D_MODEL), pp_all, i, j, (w_glu_a, w_glu_b), *tail_w,
                        mixer="ssm", tm=tm_p)
            h_s = _tail(h_s, g_s, ps_all, i, j, (w_glu_a, w_glu_b), *tail_w, mixer="ssm", tm=tm_s)
    kv5 = lambda a, b, t: a.reshape(N_ATTN_LAYERS, b, t, N_HEADS, V_DIM)
    return (h_p.reshape(batch, seq, D_MODEL), h_s.reshape(db, n_tok, D_MODEL),
            kv5(kv_p[0], batch, seq), kv5(kv_p[1], batch, seq), kv5(kv_s[0], db, n_tok), kv5(kv_s[1], db, n_tok),
            jnp.stack(states["p_re"]), jnp.stack(states["p_im"]),
            jnp.stack(states["s_re"]), jnp.stack(states["s_im"]))
```
